```python
import math
import jax
import jax.numpy as jnp
from jax import lax
import numpy as np

D_MODEL = 1024
BATCH = 8
SEQ = 4096
DEPTH = 2

CHUNK = 64
Q_BLOCK = 128
LN_EPS = 1e-5
DEEPNORM_ALPHA = (2 * DEPTH) ** 0.25
DEEPNORM_BETA = (8 * DEPTH) ** -0.25

D_FF = 2816
FFN_RES = 0.5

H_A = 8
HD_A = 64
ROT_A = HD_A // 4
ROPE_THETA = 500000.0
A_W = H_A * 2 * HD_A

H_B = 16
P_B = 64
D_INNER = H_B * P_B
G_B = 2
N_B = 128
CONV_B = 4
CONV_DIM = D_INNER + 2 * G_B * N_B

H_C = 16
HD_C = 64
C_W = H_C * HD_C

N_BRANCH = 3
BR_W = 1024

_SEGMENTS = (A_W, A_W, A_W, D_INNER, CONV_DIM, H_B, C_W, C_W, C_W, N_BRANCH * D_MODEL)
SPLIT_POINTS = tuple(int(v) for v in np.cumsum(_SEGMENTS)[:-1])
N_IN = int(sum(_SEGMENTS))

kernel_name = 'hybrid_diffattn_ssd_stickbreak_macaron'


def _layernorm(x, g, b):
    xf = x.astype(jnp.float32)
    mu = jnp.mean(xf, -1, keepdims=True)
    xc = xf - mu
    var = jnp.mean(xc * xc, -1, keepdims=True)
    y = xc * lax.rsqrt(var + LN_EPS) * g.astype(jnp.float32) + b.astype(jnp.float32)
    return y.astype(x.dtype)


def _rmsnorm(x, g):
    xf = x.astype(jnp.float32)
    y = xf * lax.rsqrt(jnp.mean(xf * xf, -1, keepdims=True) + LN_EPS)
    return (y * g.astype(jnp.float32)).astype(x.dtype)


def _swiglu(x, w_in, w_out):
    gate, up = jnp.split(x @ w_in, 2, axis=-1)
    return (jax.nn.silu(gate) * up) @ w_out


def _rotary_tables(s):
    half = ROT_A // 2
    inv_freq = ROPE_THETA ** (-jnp.arange(half, dtype=jnp.float32) * 2.0 / ROT_A)
    ang = jnp.arange(s, dtype=jnp.float32)[:, None] * inv_freq[None, :]
    return jnp.cos(ang), jnp.sin(ang)


def _apply_partial_rope(x, cos, sin):
    half = ROT_A // 2
    c = cos.reshape(cos.shape[0], 1, 1, half).astype(x.dtype)
    sn = sin.reshape(sin.shape[0], 1, 1, half).astype(x.dtype)
    x1 = x[..., :half]
    x2 = x[..., half:ROT_A]
    return jnp.concatenate([x1 * c - x2 * sn, x2 * c + x1 * sn, x[..., ROT_A:]], axis=-1)


def _diff_attention(qa, ka, va, lam, subln_g, layer):
    b, s, _ = qa.shape
    nb = s // Q_BLOCK
    cos, sin = _rotary_tables(s)
    q = _apply_partial_rope(qa.reshape(b, s, H_A, 2, HD_A), cos, sin)
    k = _apply_partial_rope(ka.reshape(b, s, H_A, 2, HD_A), cos, sin)
    v = va.reshape(b, s, H_A, 2 * HD_A)
    lam_init = 0.8 - 0.6 * math.exp(-0.3 * layer)
    lf = lam.astype(jnp.float32)
    lam_full = jnp.exp(jnp.sum(lf[0] * lf[1])) - jnp.exp(jnp.sum(lf[2] * lf[3])) + lam_init
    kpos = jnp.arange(s)
    kchunk = kpos // CHUNK
    qpos = kpos.reshape(nb, Q_BLOCK)
    qb = jnp.moveaxis(q.reshape(b, nb, Q_BLOCK, H_A, 2, HD_A), 1, 0)
    scale = HD_A ** -0.5

    def block(args):
        qblk, qp = args
        sc = jnp.einsum('bqhmd,bkhmd->bhmqk', qblk, k).astype(jnp.float32) * scale
        allowed = kchunk[None, :] <= (qp // CHUNK)[:, None]
        p = jax.nn.softmax(jnp.where(allowed, sc, -jnp.inf), axis=-1)
        w = p[:, :, 0] - lam_full * p[:, :, 1]
        return jnp.einsum('bhqk,bkhe->bqhe', w.astype(v.dtype), v)

    out = lax.map(block, (qb, qpos))
    out = jnp.moveaxis(out, 0, 1).reshape(b, s, H_A, 2 * HD_A)
    out = _rmsnorm(out, subln_g) * (1.0 - lam_init)
    return out.reshape(b, s, A_W)


def _stick_breaking(qc, kc, vc):
    b, s, _ = qc.shape
    nb = s // Q_BLOCK
    k = kc.reshape(b, s, H_C, HD_C)
    v = vc.reshape(b, s, H_C, HD_C)
    qb = jnp.moveaxis(qc.reshape(b, nb, Q_BLOCK, H_C, HD_C), 1, 0)
    kpos = jnp.arange(s)
    qpos = kpos.reshape(nb, Q_BLOCK)
    scale = HD_C ** -0.5

    def block(args):
        qblk, qp = args
        z = jnp.einsum('bqhd,bkhd->bhqk', qblk, k).astype(jnp.float32) * scale
        strict = kpos[None, :] < qp[:, None]
        log_stay = jnp.where(strict, jax.nn.log_sigmoid(-z), 0.0)
        after = lax.cumsum(log_stay, axis=3, reverse=True) - log_stay
        a = jnp.where(strict, jnp.exp(jax.nn.log_sigmoid(z) + after), 0.0)
        return jnp.einsum('bhqk,bkhd->bqhd', a.astype(v.dtype), v)

    out = lax.map(block, (qb, qpos))
    return jnp.moveaxis(out, 0, 1).reshape(b, s, C_W)


def _ssd(x, dt, a_neg, bm, cm):
    b, s = x.shape[:2]
    nc = s // CHUNK
    hg = H_B // G_B
    xr = x.reshape(b, nc, CHUNK, G_B, hg, P_B)
    dtr = dt.reshape(b, nc, CHUNK, G_B, hg)
    br = bm.reshape(b, nc, CHUNK, G_B, N_B)
    cr = cm.reshape(b, nc, CHUNK, G_B, N_B)
    acum = jnp.cumsum(dtr * a_neg.reshape(G_B, hg), axis=2)
    xdt = xr * dtr[..., None]
    causal = jnp.tril(jnp.ones((CHUNK, CHUNK), dtype=bool))[None, None, :, :, None, None]
    seg = acum[:, :, :, None] - acum[:, :, None, :]
    decay = jnp.exp(jnp.where(causal, seg, -jnp.inf))
    cb = jnp.einsum('bctgn,bcsgn->bctsg', cr, br)
    y_diag = jnp.einsum('bctsgh,bcsghp->bctghp', cb[..., None] * decay, xdt)
    decay_to_end = jnp.exp(acum[:, :, -1:] - acum)
    states = jnp.einsum('bclgn,bclghp->bcghpn', br, xdt * decay_to_end[..., None])
    chunk_decay = jnp.exp(acum[:, :, -1])

    def step(h_prev, inp):
        st, dc = inp
        return h_prev * dc[..., None, None] + st, h_prev

    h0 = jnp.zeros((b, G_B, hg, P_B, N_B), jnp.float32)
    _, h_in = lax.scan(step, h0, (jnp.moveaxis(states, 1, 0), jnp.moveaxis(chunk_decay, 1, 0)))
    h_in = jnp.moveaxis(h_in, 0, 1)
    y_off = jnp.einsum('bclgn,bcghpn->bclghp', cr, h_in) * jnp.exp(acum)[..., None]
    return (y_diag + y_off).reshape(b, s, H_B, P_B)


def _mamba2(zb, xbc, dtb, conv_w, conv_b, dt_bias, a_log, d_skip, norm_g):
    b, s, _ = zb.shape
    xbc = lax.conv_general_dilated(
        xbc, conv_w[:, None, :], window_strides=(1,), padding=[(CONV_B - 1, 0)],
        dimension_numbers=('NWC', 'WIO', 'NWC'), feature_group_count=CONV_DIM) + conv_b
    xbc = jax.nn.silu(xbc).astype(jnp.float32)
    xs, bm, cm = jnp.split(xbc, [D_INNER, D_INNER + G_B * N_B], axis=-1)
    dt = jax.nn.softplus(dtb.astype(jnp.float32) + dt_bias.astype(jnp.float32))
    a_neg = -jnp.exp(a_log.astype(jnp.float32))
    xh = xs.reshape(b, s, H_B, P_B)
    y = _ssd(xh, dt, a_neg, bm.reshape(b, s, G_B, N_B), cm.reshape(b, s, G_B, N_B))
    y = y + d_skip.astype(jnp.float32)[:, None] * xh
    y = y.reshape(b, s, D_INNER) * jax.nn.silu(zb.astype(jnp.float32))
    y = _rmsnorm(y.reshape(b, s, G_B, D_INNER // G_B), norm_g.reshape(G_B, D_INNER // G_B))
    return y.reshape(b, s, D_INNER).astype(zb.dtype)


def _hybrid_mixer(h, layer, w_in, gate_bias, diff_lambda, diff_subln_g, conv_w, conv_b,
                  dt_bias, a_log, d_skip, norm_g, w_branch, w_out):
    b, s, _ = h.shape
    qa, ka, va, zb, xbc, dtb, qc, kc, vc, gpre = jnp.split(h @ w_in, SPLIT_POINTS, axis=-1)
    o_a = _diff_attention(qa, ka, va, diff_lambda, diff_subln_g, layer)
    o_b = _mamba2(zb, xbc, dtb, conv_w, conv_b, dt_bias, a_log, d_skip, norm_g)
    o_c = _stick_breaking(qc, kc, vc)
    branches = jnp.stack([o_a, o_b.astype(h.dtype), o_c], axis=2)
    proj = jnp.einsum('bsrc,rcd->bsrd', branches, w_branch)
    gates = jax.nn.sigmoid(gpre.reshape(b, s, N_BRANCH, D_MODEL) + gate_bias)
    merged = jnp.einsum('bsrd,bsrd->bsd', gates, proj)
    return merged @ w_out


def setup_inputs(seed: int = 0) -> dict:
    key = jax.random.key(seed)
    ks = jax.random.split(key, 24)
    f32 = jnp.float32
    L = DEPTH

    def nrm(k, shape, scale):
        return jax.random.normal(k, shape, f32) * scale

    dt0 = jnp.exp(jax.random.uniform(ks[11], (L, H_B), f32, math.log(1e-3), math.log(1e-1)))
    return {
        'x': nrm(ks[0], (BATCH, SEQ, D_MODEL), 1.0),
        'ffn1_w_in': nrm(ks[1], (L, D_MODEL, 2 * D_FF), DEEPNORM_BETA * D_MODEL ** -0.5),
        'ffn1_w_out': nrm(ks[2], (L, D_FF, D_MODEL), DEEPNORM_BETA * D_FF ** -0.5),
        'ln1_g': 1.0 + nrm(ks[3], (L, D_MODEL), 0.02),
        'ln1_b': nrm(ks[4], (L, D_MODEL), 0.02),
        'w_mix_in': nrm(ks[5], (L, D_MODEL, N_IN), D_MODEL ** -0.5),
        'gate_bias': nrm(ks[6], (L, N_BRANCH, D_MODEL), 0.02),
        'diff_lambda': nrm(ks[7], (L, 4, HD_A), 0.1),
        'diff_subln_g': 1.0 + nrm(ks[8], (L, 2 * HD_A), 0.02),
        'ssm_conv_w': nrm(ks[9], (L, CONV_B, CONV_DIM), CONV_B ** -0.5),
        'ssm_conv_b': nrm(ks[10], (L, CONV_DIM), 0.02),
        'ssm_dt_bias': dt0 + jnp.log(-jnp.expm1(-dt0)),
        'ssm_A_log': jnp.log(jax.random.uniform(ks[12], (L, H_B), f32, 1.0, 16.0)),
        'ssm_D': 1.0 + nrm(ks[13], (L, H_B), 0.02),
        'ssm_norm_g': 1.0 + nrm(ks[14], (L, D_INNER), 0.02),
        'w_branch': nrm(ks[15], (L, N_BRANCH, BR_W, D_MODEL), DEEPNORM_BETA * BR_W ** -0.5),
        'w_mix_out': nrm(ks[16], (L, D_MODEL, D_MODEL), DEEPNORM_BETA * D_MODEL ** -0.5),
        'ln2_g': 1.0 + nrm(ks[17], (L, D_MODEL), 0.02),
        'ln2_b': nrm(ks[18], (L, D_MODEL), 0.02),
        'ffn2_w_in': nrm(ks[19], (L, D_MODEL, 2 * D_FF), DEEPNORM_BETA * D_MODEL ** -0.5),
        'ffn2_w_out': nrm(ks[20], (L, D_FF, D_MODEL), DEEPNORM_BETA * D_FF ** -0.5),
        'ln3_g': 1.0 + nrm(ks[21], (L, D_MODEL), 0.02),
        'ln3_b': nrm(ks[22], (L, D_MODEL), 0.02),
    }


def reference(x, ffn1_w_in, ffn1_w_out, ln1_g, ln1_b, w_mix_in, gate_bias, diff_lambda,
              diff_subln_g, ssm_conv_w, ssm_conv_b, ssm_dt_bias, ssm_A_log, ssm_D, ssm_norm_g,
              w_branch, w_mix_out, ln2_g, ln2_b, ffn2_w_in, ffn2_w_out, ln3_g, ln3_b):
    for l in range(DEPTH):
        x = _layernorm(DEEPNORM_ALPHA * x + FFN_RES * _swiglu(x, ffn1_w_in[l], ffn1_w_out[l]),
                       ln1_g[l], ln1_b[l])
        mix = _hybrid_mixer(x, l, w_mix_in[l], gate_bias[l], diff_lambda[l], diff_subln_g[l],
                            ssm_conv_w[l], ssm_conv_b[l], ssm_dt_bias[l], ssm_A_log[l], ssm_D[l],
                            ssm_norm_g[l], w_branch[l], w_mix_out[l])
        x = _layernorm(DEEPNORM_ALPHA * x + mix, ln2_g[l], ln2_b[l])
        x = _layernorm(DEEPNORM_ALPHA * x + FFN_RES * _swiglu(x, ffn2_w_in[l], ffn2_w_out[l]),
                       ln3_g[l], ln3_b[l])
    return x
```

```python
import functools
import math

import jax
import jax.numpy as jnp
from jax import lax
from jax.experimental import pallas as pl
from jax.experimental.pallas import tpu as pltpu

F32 = jnp.float32
BF16 = jnp.bfloat16

D_MODEL = 1024
DEPTH = 2
LN_EPS = 1e-5
DEEPNORM_ALPHA = (2 * DEPTH) ** 0.25
D_FF = 2816
FFN_RES = 0.5

CHUNK = 64
H_A = 8
HD_A = 64
ROT_A = HD_A // 4
ROPE_THETA = 500000.0
A_W = H_A * 2 * HD_A

H_B = 16
P_B = 64
D_INNER = H_B * P_B
G_B = 2
N_B = 128
CONV_B = 4
CONV_DIM = D_INNER + 2 * G_B * N_B

H_C = 16
HD_C = 64
C_W = H_C * HD_C
N_BRANCH = 3

LANES = 128
VMEM_LIMIT = 56 * 1024 * 1024

FFN_TM = 512
FFN_CHUNK = 256
PROJ_TM = 512
PROJ_TN = 1024
MERGE_TM = 512
ATT_A_T = 512
ATT_C_T = 256
SSD_L = 256

NEG_BIG = -1e30


def _cparams(sem):
    return pltpu.CompilerParams(dimension_semantics=sem, vmem_limit_bytes=VMEM_LIMIT)


def _dot(a, b):
    return jnp.dot(a, b, preferred_element_type=F32)


def _dot_nt(a, b):
    return lax.dot_general(a, b, (((1,), (1,)), ((), ())), preferred_element_type=F32)


def _dot_tn(a, b):
    return lax.dot_general(a, b, (((0,), (0,)), ((), ())), preferred_element_type=F32)


def _split3(x):
    hi = x.astype(BF16)
    r = x - hi.astype(F32)
    mid = r.astype(BF16)
    lo = (r - mid.astype(F32)).astype(BF16)
    return hi, mid, lo


def _dot_exact_rhs(a_bf16, x):
    hi, mid, lo = _split3(x)
    return _dot(a_bf16, hi) + (_dot(a_bf16, mid) + _dot(a_bf16, lo))


def _dot_exact_lhs(x, a_bf16):
    hi, mid, lo = _split3(x)
    return _dot(hi, a_bf16) + (_dot(mid, a_bf16) + _dot(lo, a_bf16))


def _layernorm(y, g, b):
    mu = jnp.mean(y, axis=-1, keepdims=True)
    yc = y - mu
    var = jnp.mean(yc * yc, axis=-1, keepdims=True)
    return yc * lax.rsqrt(var + LN_EPS) * g + b


def _softplus(x):
    return jnp.maximum(x, 0.0) + jnp.log(1.0 + jnp.exp(-jnp.abs(x)))


def _ffn_ln_kernel(x_ref, win_ref, wout_ref, g_ref, b_ref, o_ref):
    x = x_ref[...]
    xb = x.astype(BF16)
    acc = jnp.zeros(x.shape, F32)
    for c in range(D_FF // FFN_CHUNK):
        lo = c * FFN_CHUNK
        gate = _dot(xb, win_ref[:, lo:lo + FFN_CHUNK])
        up = _dot(xb, win_ref[:, D_FF + lo:D_FF + lo + FFN_CHUNK])
        a = (gate * jax.nn.sigmoid(gate) * up).astype(BF16)
        acc = acc + _dot(a, wout_ref[lo:lo + FFN_CHUNK, :])
    y = DEEPNORM_ALPHA * x + FFN_RES * acc
    o_ref[...] = _layernorm(y, g_ref[...], b_ref[...])


def _ffn_ln(x, w_in, w_out, g, b):
    t = x.shape[0]
    tm = min(FFN_TM, t)
    const = lambda i: (0, 0)
    return pl.pallas_call(
        _ffn_ln_kernel,
        grid=(t // tm,),
        in_specs=[
            pl.BlockSpec((tm, D_MODEL), lambda i: (i, 0)),
            pl.BlockSpec((D_MODEL, 2 * D_FF), const, pipeline_mode=pl.Buffered(1)),
            pl.BlockSpec((D_FF, D_MODEL), const, pipeline_mode=pl.Buffered(1)),
            pl.BlockSpec((1, D_MODEL), const),
            pl.BlockSpec((1, D_MODEL), const),
        ],
        out_specs=pl.BlockSpec((tm, D_MODEL), lambda i: (i, 0)),
        out_shape=jax.ShapeDtypeStruct((t, D_MODEL), F32),
        compiler_params=_cparams(("parallel",)),
        name="ffn_ln",
    )(x, w_in, w_out, g.reshape(1, D_MODEL), b.reshape(1, D_MODEL))


def _proj_kernel(x_ref, w_ref, o_ref):
    o_ref[...] = _dot(x_ref[...].astype(BF16), w_ref[...]).astype(o_ref.dtype)


def _proj(x, w, out_dtype, tn):
    t = x.shape[0]
    n = w.shape[1]
    tm = min(PROJ_TM, t)
    return pl.pallas_call(
        _proj_kernel,
        grid=(t // tm, n // tn),
        in_specs=[
            pl.BlockSpec((tm, D_MODEL), lambda i, j: (i, 0)),
            pl.BlockSpec((D_MODEL, tn), lambda i, j: (0, j)),
        ],
        out_specs=pl.BlockSpec((tm, tn), lambda i, j: (i, j)),
        out_shape=jax.ShapeDtypeStruct((t, n), out_dtype),
        compiler_params=_cparams(("parallel", "parallel")),
        name="proj",
    )(x, w)


def _proj_rope_kernel(x_ref, w_ref, c_ref, s1_ref, s2_ref, o_ref):
    acc = _dot(x_ref[...].astype(BF16), w_ref[...])
    c = c_ref[...]
    s1 = s1_ref[...]
    s2 = s2_ref[...]
    for h in range(acc.shape[1] // LANES):
        a = acc[:, h * LANES:(h + 1) * LANES]
        r = a * c + pltpu.roll(a, 8, axis=1) * s1 + pltpu.roll(a, LANES - 8, axis=1) * s2
        o_ref[:, h * LANES:(h + 1) * LANES] = r.astype(o_ref.dtype)


def _proj_rope(x, w, tabs, seq, tn):
    t = x.shape[0]
    n = w.shape[1]
    tm = min(PROJ_TM, seq)
    nrow = seq // tm
    tab_spec = pl.BlockSpec((tm, LANES), lambda i, j: (i % nrow, 0))
    return pl.pallas_call(
        _proj_rope_kernel,
        grid=(t // tm, n // tn),
        in_specs=[
            pl.BlockSpec((tm, D_MODEL), lambda i, j: (i, 0)),
            pl.BlockSpec((D_MODEL, tn), lambda i, j: (0, j)),
            tab_spec, tab_spec, tab_spec,
        ],
        out_specs=pl.BlockSpec((tm, tn), lambda i, j: (i, j)),
        out_shape=jax.ShapeDtypeStruct((t, n), BF16),
        compiler_params=_cparams(("parallel", "parallel")),
        name="proj_rope",
    )(x, w, *tabs)


def _rope_tables(seq):
    half = ROT_A // 2
    inv_freq = ROPE_THETA ** (-jnp.arange(half, dtype=F32) * 2.0 / ROT_A)
    ang = jnp.arange(seq, dtype=F32)[:, None] * inv_freq[None, :]
    cos, sin = jnp.cos(ang), jnp.sin(ang)
    ones = jnp.ones((seq, HD_A - ROT_A), F32)
    zeros = jnp.zeros((seq, HD_A - ROT_A), F32)
    zh = jnp.zeros((seq, half), F32)
    c = jnp.concatenate([cos, cos, ones], axis=1)
    s1 = jnp.concatenate([zh, sin, zeros], axis=1)
    s2 = jnp.concatenate([-sin, zh, zeros], axis=1)
    rep = LANES // HD_A
    return tuple(jnp.tile(v, (1, rep)) for v in (c, s1, s2))


def _diff_attn_kernel(q_ref, k_ref, v_ref, lam_ref, g_ref, o_ref, *, lam_init):
    t = q_ref.shape[0]
    i = pl.program_id(2)
    lane = lax.broadcasted_iota(jnp.int32, (t, LANES), 1)
    q = q_ref[...] * jnp.asarray(HD_A ** -0.5, BF16)
    zero = jnp.zeros_like(q)
    qq = jnp.concatenate([jnp.where(lane < HD_A, q, zero), jnp.where(lane < HD_A, zero, q)], axis=0)

    def step(j, carry, masked):
        m, l, acc = carry
        start = pl.multiple_of(j * t, t)
        kb = k_ref[pl.ds(start, t), :]
        vb = v_ref[pl.ds(start, t), :]
        s = _dot_nt(qq, kb)
        if masked:
            qchunk = lax.broadcasted_iota(jnp.int32, (2 * t, t), 0) % t // CHUNK
            kchunk = lax.broadcasted_iota(jnp.int32, (2 * t, t), 1) // CHUNK
            s = jnp.where(kchunk <= qchunk, s, NEG_BIG)
        m_new = jnp.maximum(m, jnp.max(s, axis=1, keepdims=True))
        p = jnp.exp(s - m_new)
        corr = jnp.exp(m - m_new)
        l = corr * l + jnp.sum(p, axis=1, keepdims=True)
        acc = corr * acc + _dot(p.astype(BF16), vb)
        return m_new, l, acc

    init = (jnp.full((2 * t, 1), NEG_BIG, F32), jnp.zeros((2 * t, 1), F32),
            jnp.zeros((2 * t, LANES), F32))
    carry = lax.fori_loop(0, i, lambda j, c: step(j, c, False), init)
    m, l, acc = step(i, carry, True)
    o = acc / l
    lf = lam_ref[...]
    lam_full = (jnp.exp(jnp.sum(lf[0:1] * lf[1:2], axis=1, keepdims=True))
                - jnp.exp(jnp.sum(lf[2:3] * lf[3:4], axis=1, keepdims=True)) + lam_init)
    d = o[:t] - lam_full * o[t:]
    y = d * lax.rsqrt(jnp.mean(d * d, axis=-1, keepdims=True) + LN_EPS) * g_ref[...]
    o_ref[...] = (y * (1.0 - lam_init)).astype(o_ref.dtype)


def _diff_attn(qk, v_arr, v_col0, lam, subln_g, layer, batch, seq):
    t = min(ATT_A_T, seq)
    lam_init = 0.8 - 0.6 * math.exp(-0.3 * layer)
    kern = functools.partial(_diff_attn_kernel, lam_init=lam_init)
    return pl.pallas_call(
        kern,
        grid=(batch, H_A, seq // t),
        in_specs=[
            pl.BlockSpec((None, t, LANES), lambda b, h, i: (b, i, h)),
            pl.BlockSpec((None, seq, LANES), lambda b, h, i: (b, 0, H_A + h)),
            pl.BlockSpec((None, seq, LANES), lambda b, h, i: (b, 0, v_col0 + h)),
            pl.BlockSpec((4, HD_A), lambda b, h, i: (0, 0)),
            pl.BlockSpec((1, 2 * HD_A), lambda b, h, i: (0, 0)),
        ],
        out_specs=pl.BlockSpec((None, t, LANES), lambda b, h, i: (b, i, h)),
        out_shape=jax.ShapeDtypeStruct((batch, seq, A_W), BF16),
        compiler_params=_cparams(("parallel", "parallel", "parallel")),
        name="diff_attn",
    )(qk, qk, v_arr, lam, subln_g.reshape(1, 2 * HD_A))


def _stick_kernel(q_ref, k_ref, v_ref, o_ref):
    t = q_ref.shape[0]
    i = pl.program_id(2)
    lane = lax.broadcasted_iota(jnp.int32, (t, LANES), 1)
    q = q_ref[...] * jnp.asarray(HD_C ** -0.5, BF16)
    zero = jnp.zeros_like(q)
    qq = jnp.concatenate([jnp.where(lane < HD_C, q, zero), jnp.where(lane < HD_C, zero, q)], axis=0)
    u = jnp.where(lax.broadcasted_iota(jnp.int32, (t, t), 0) > lax.broadcasted_iota(jnp.int32, (t, t), 1),
                  -1.0, 0.0).astype(BF16)

    def step(j, carry, masked):
        run, acc = carry
        start = pl.multiple_of(j * t, t)
        kb = k_ref[pl.ds(start, t), :]
        vb = v_ref[pl.ds(start, t), :]
        z = _dot_nt(qq, kb)
        sp = _softplus(z)
        if masked:
            strict = (lax.broadcasted_iota(jnp.int32, (2 * t, t), 1)
                      < lax.broadcasted_iota(jnp.int32, (2 * t, t), 0) % t)
            sp_m = jnp.where(strict, sp, 0.0)
        else:
            sp_m = sp
        hi = sp_m.astype(BF16)
        lo = (sp_m - hi.astype(F32)).astype(BF16)
        after = _dot(hi, u) + _dot(lo, u)
        a = jnp.exp((z - sp) + after + run)
        if masked:
            a = jnp.where(strict, a, 0.0)
        acc = acc + _dot(a.astype(BF16), vb)
        run = run - jnp.sum(sp_m, axis=1, keepdims=True)
        return run, acc

    init = (jnp.zeros((2 * t, 1), F32), jnp.zeros((2 * t, LANES), F32))
    carry = step(i, init, True)
    _, acc = lax.fori_loop(0, i, lambda n, c: step(i - 1 - n, c, False), carry)
    o_ref[...] = jnp.where(lane < HD_C, acc[:t], acc[t:]).astype(o_ref.dtype)


def _stick_breaking(arr, q_col0, k_col0, v_col0, batch, seq):
    t = min(ATT_C_T, seq)
    nhp = C_W // LANES
    return pl.pallas_call(
        _stick_kernel,
        grid=(batch, nhp, seq // t),
        in_specs=[
            pl.BlockSpec((None, t, LANES), lambda b, h, i: (b, i, q_col0 + h)),
            pl.BlockSpec((None, seq, LANES), lambda b, h, i: (b, 0, k_col0 + h)),
            pl.BlockSpec((None, seq, LANES), lambda b, h, i: (b, 0, v_col0 + h)),
        ],
        out_specs=pl.BlockSpec((None, t, LANES), lambda b, h, i: (b, i, h)),
        out_shape=jax.ShapeDtypeStruct((batch, seq, C_W), BF16),
        compiler_params=_cparams(("parallel", "parallel", "parallel")),
        name="stick_breaking",
    )(arr, arr, arr)


HALO = 8


def _ssd_kernel(z_ref, x_ref, bc_ref, dt_ref, cw_ref, cb_ref, dtb_ref, alog_ref, dskip_ref, ng_ref,
                o_ref, xe_ref, st_ref):
    L = z_ref.shape[0]
    c = pl.program_id(1)

    @pl.when(c == 0)
    def _():
        xe_ref[0:HALO, :] = jnp.zeros((HALO, CONV_DIM), F32)
        st_ref[...] = jnp.zeros(st_ref.shape, F32)

    xe_ref[HALO:HALO + L, 0:D_INNER] = x_ref[...]
    xe_ref[HALO:HALO + L, D_INNER:CONV_DIM] = bc_ref[...]
    conv = cb_ref[...]
    for r in range(CONV_B):
        off = HALO - (CONV_B - 1) + r
        conv = conv + cw_ref[r:r + 1, :] * xe_ref[off:off + L, :]
    xe_ref[0:HALO, :] = xe_ref[L:L + HALO, :]
    xc = conv * jax.nn.sigmoid(conv)
    xs = xc[:, :D_INNER]

    dt = _softplus(dt_ref[...] + dtb_ref[...])
    a = dt * (-jnp.exp(alog_ref[...]))
    row_i = lax.broadcasted_iota(jnp.int32, (L, L), 0)
    col_i = lax.broadcasted_iota(jnp.int32, (L, L), 1)
    causal = col_i <= row_i
    tri = jnp.where(causal, 1.0, 0.0).astype(BF16)
    acum = _dot_exact_rhs(tri, a)
    acum_t = acum.T
    a_last = acum[L - 1:L, :]
    e_in = jnp.exp(acum)
    e_out = jnp.exp(a_last - acum)

    expand = jnp.where(lax.broadcasted_iota(jnp.int32, (LANES, D_INNER), 0)
                       == lax.broadcasted_iota(jnp.int32, (LANES, D_INNER), 1) // P_B,
                       1.0, 0.0).astype(BF16)
    dt_x = _dot_exact_lhs(dt, expand)
    e_in_x = _dot_exact_lhs(e_in, expand)
    e_out_x = _dot_exact_lhs(e_out, expand)

    xdt = xs * dt_x
    xdt_b = xdt.astype(BF16)
    xdt_out_b = (xdt * e_out_x).astype(BF16)
    lane = lax.broadcasted_iota(jnp.int32, (L, LANES), 1)

    hg = H_B // G_B
    gw = D_INNER // G_B
    y_parts = []
    for g in range(G_B):
        bm_f = xc[:, D_INNER + g * N_B:D_INNER + (g + 1) * N_B]
        bm = bm_f.astype(BF16)
        bm_t = bm_f.T.astype(BF16)
        cm = xc[:, D_INNER + G_B * N_B + g * N_B:D_INNER + G_B * N_B + (g + 1) * N_B].astype(BF16)
        cb = _dot_nt(cm, bm)
        st_g = st_ref[:, g * gw:(g + 1) * gw]
        y_off = _dot(cm, st_g.astype(BF16)) * e_in_x[:, g * gw:(g + 1) * gw]
        y_diag = []
        for hp in range(hg // 2):
            h0 = g * hg + 2 * hp
            pair = xdt_b[:, h0 * P_B:(h0 + 2) * P_B]
            ys = []
            for h in (h0, h0 + 1):
                seg = acum[:, h:h + 1] - acum_t[h:h + 1, :]
                m = (cb * jnp.exp(jnp.where(causal, seg, -jnp.inf))).astype(BF16)
                ys.append(_dot(m, pair))
            y_diag.append(jnp.where(lane < P_B, ys[0], ys[1]))
        y_parts.append(jnp.concatenate(y_diag, axis=1) + y_off)
        upd = _dot(bm_t, xdt_out_b[:, g * gw:(g + 1) * gw])
        st_ref[:, g * gw:(g + 1) * gw] = st_g * e_in_x[L - 1:L, g * gw:(g + 1) * gw] + upd

    y = jnp.concatenate(y_parts, axis=1) + dskip_ref[...] * xs
    zz = z_ref[...]
    y = y * (zz * jax.nn.sigmoid(zz))
    outs = []
    for g in range(G_B):
        yg = y[:, g * gw:(g + 1) * gw]
        outs.append(yg * lax.rsqrt(jnp.mean(yg * yg, axis=-1, keepdims=True) + LN_EPS))
    o_ref[...] = (jnp.concatenate(outs, axis=1) * ng_ref[...]).astype(o_ref.dtype)


def _ssd(ssm, conv_w, conv_b, dt_bias, a_log, d_skip, norm_g, batch, seq):
    L = min(SSD_L, seq)
    pad = LANES - H_B
    row = lambda v: v.reshape(1, -1).astype(F32)
    dtb = jnp.pad(row(dt_bias), ((0, 0), (0, pad)))
    alog = jnp.pad(row(a_log), ((0, 0), (0, pad)))
    dskip = jnp.repeat(row(d_skip), P_B, axis=1)
    const = lambda b, c: (0, 0)
    bc_w = CONV_DIM - D_INNER
    return pl.pallas_call(
        _ssd_kernel,
        grid=(batch, seq // L),
        in_specs=[
            pl.BlockSpec((None, L, D_INNER), lambda b, c: (b, c, 0)),
            pl.BlockSpec((None, L, D_INNER), lambda b, c: (b, c, 1)),
            pl.BlockSpec((None, L, bc_w), lambda b, c: (b, c, 2 * D_INNER // bc_w)),
            pl.BlockSpec((None, L, LANES), lambda b, c: (b, c, (D_INNER + CONV_DIM) // LANES)),
            pl.BlockSpec((CONV_B, CONV_DIM), const),
            pl.BlockSpec((1, CONV_DIM), const),
            pl.BlockSpec((1, LANES), const),
            pl.BlockSpec((1, LANES), const),
            pl.BlockSpec((1, D_INNER), const),
            pl.BlockSpec((1, D_INNER), const),
        ],
        out_specs=pl.BlockSpec((None, L, D_INNER), lambda b, c: (b, c, 0)),
        out_shape=jax.ShapeDtypeStruct((batch, seq, D_INNER), BF16),
        scratch_shapes=[pltpu.VMEM((HALO + L, CONV_DIM), F32), pltpu.VMEM((N_B, D_INNER), F32)],
        compiler_params=_cparams(("parallel", "arbitrary")),
        name="ssd",
    )(ssm, ssm, ssm, ssm, conv_w, row(conv_b), dtb, alog, dskip, row(norm_g))


def _merge_ln_kernel(h_ref, oa_ref, ob_ref, oc_ref, wg_ref, gb_ref, wbr_ref, wout_ref, g_ref, b_ref, o_ref):
    h = h_ref[...]
    hb = h.astype(BF16)
    merged = jnp.zeros(h.shape, F32)
    for r, br_ref in enumerate((oa_ref, ob_ref, oc_ref)):
        gpre = _dot(hb, wg_ref[:, r * D_MODEL:(r + 1) * D_MODEL]) + gb_ref[r:r + 1, :]
        merged = merged + jax.nn.sigmoid(gpre) * _dot(br_ref[...], wbr_ref[r])
    mix = _dot(merged.astype(BF16), wout_ref[...])
    o_ref[...] = _layernorm(DEEPNORM_ALPHA * h + mix, g_ref[...], b_ref[...])


def _merge_ln(h, oa, ob, oc, w_gate, gate_bias, w_branch, w_out, g, b):
    t = h.shape[0]
    tm = min(MERGE_TM, t)
    rowspec = pl.BlockSpec((tm, D_MODEL), lambda i: (i, 0))
    const2 = lambda i: (0, 0)
    one = pl.Buffered(1)
    return pl.pallas_call(
        _merge_ln_kernel,
        grid=(t // tm,),
        in_specs=[
            rowspec, rowspec, rowspec, rowspec,
            pl.BlockSpec((D_MODEL, N_BRANCH * D_MODEL), const2, pipeline_mode=one),
            pl.BlockSpec((N_BRANCH, D_MODEL), const2),
            pl.BlockSpec((N_BRANCH, D_MODEL, D_MODEL), lambda i: (0, 0, 0), pipeline_mode=one),
            pl.BlockSpec((D_MODEL, D_MODEL), const2, pipeline_mode=one),
            pl.BlockSpec((1, D_MODEL), const2),
            pl.BlockSpec((1, D_MODEL), const2),
        ],
        out_specs=rowspec,
        out_shape=jax.ShapeDtypeStruct((t, D_MODEL), F32),
        compiler_params=_cparams(("parallel",)),
        name="merge_ln",
    )(h, oa, ob, oc, w_gate, gate_bias, w_branch, w_out, g.reshape(1, D_MODEL), b.reshape(1, D_MODEL))


def _mixer_weights(w_in):
    segs = (A_W, A_W, A_W, D_INNER, CONV_DIM, H_B, C_W, C_W, C_W, N_BRANCH * D_MODEL)
    offs = [0]
    for s in segs:
        offs.append(offs[-1] + s)
    qa, ka, va, zb, xbc, dtb, qc, kc, vc, gate = (w_in[:, offs[n]:offs[n + 1]] for n in range(len(segs)))
    w_qk = jnp.concatenate([qa, ka], axis=1).astype(BF16)
    w_att = jnp.concatenate([va, qc, kc, vc], axis=1).astype(BF16)
    w_ssm = jnp.concatenate([zb, xbc, jnp.pad(dtb, ((0, 0), (0, LANES - H_B)))], axis=1).astype(BF16)
    return w_qk, w_att, w_ssm, gate.astype(BF16)


def kernel(x, ffn1_w_in, ffn1_w_out, ln1_g, ln1_b, w_mix_in, gate_bias, diff_lambda, diff_subln_g,
           ssm_conv_w, ssm_conv_b, ssm_dt_bias, ssm_A_log, ssm_D, ssm_norm_g, w_branch, w_mix_out,
           ln2_g, ln2_b, ffn2_w_in, ffn2_w_out, ln3_g, ln3_b):
    batch, seq, _ = x.shape
    t = batch * seq
    tabs = _rope_tables(seq)
    xf = x.reshape(t, D_MODEL)
    for l in range(DEPTH):
        xf = _ffn_ln(xf, ffn1_w_in[l].astype(BF16), ffn1_w_out[l].astype(BF16), ln1_g[l], ln1_b[l])

        w_qk, w_att, w_ssm, w_gate = _mixer_weights(w_mix_in[l])
        qk = _proj_rope(xf, w_qk, tabs, seq, PROJ_TN).reshape(batch, seq, 2 * A_W)
        att = _proj(xf, w_att, BF16, PROJ_TN).reshape(batch, seq, A_W + 3 * C_W)
        ssm = _proj(xf, w_ssm, F32, w_ssm.shape[1] // 3).reshape(batch, seq, w_ssm.shape[1])

        o_a = _diff_attn(qk, att, 0, diff_lambda[l], diff_subln_g[l], l, batch, seq)
        nb = A_W // LANES
        o_c = _stick_breaking(att, nb, 2 * nb, 3 * nb, batch, seq)
        o_b = _ssd(ssm, ssm_conv_w[l], ssm_conv_b[l], ssm_dt_bias[l], ssm_A_log[l], ssm_D[l],
                   ssm_norm_g[l], batch, seq)

        xf = _merge_ln(xf, o_a.reshape(t, A_W), o_b.reshape(t, D_INNER), o_c.reshape(t, C_W),
                       w_gate, gate_bias[l], w_branch[l].astype(BF16), w_mix_out[l].astype(BF16),
                       ln2_g[l], ln2_b[l])

        xf = _ffn_ln(xf, ffn2_w_in[l].astype(BF16), ffn2_w_out[l].astype(BF16), ln3_g[l], ln3_b[l])
    return xf.reshape(batch, seq, D_MODEL)
```

```python
import functools
import math

import jax
import jax.numpy as jnp
from jax import lax
from jax.experimental import pallas as pl
from jax.experimental.pallas import tpu as pltpu

F32 = jnp.float32
BF16 = jnp.bfloat16

D_MODEL = 1024
DEPTH = 2
LN_EPS = 1e-5
DEEPNORM_ALPHA = (2 * DEPTH) ** 0.25
D_FF = 2816
FFN_RES = 0.5

CHUNK = 64
H_A = 8
HD_A = 64
ROT_A = HD_A // 4
ROPE_THETA = 500000.0
A_W = H_A * 2 * HD_A

H_B = 16
P_B = 64
D_INNER = H_B * P_B
G_B = 2
N_B = 128
CONV_B = 4
CONV_DIM = D_INNER + 2 * G_B * N_B

H_C = 16
HD_C = 64
C_W = H_C * HD_C
N_BRANCH = 3

LANES = 128
VMEM_LIMIT = 56 * 1024 * 1024

FFN_TM = 512
FFN_CHUNK = 256
PROJ_TM = 512
PROJ_TN = 1024
MERGE_TM = 512
ATT_A_T = 512
ATT_A_NT = 4
ATT_C_T = 512
ATT_C_TK = 256
ATT_C_NT = 4
SSD_L = 256

NEG_BIG = -1e30
LOG2E = 1.4426950408889634


def _cparams(sem):
    return pltpu.CompilerParams(dimension_semantics=sem, vmem_limit_bytes=VMEM_LIMIT)


def _dot(a, b):
    return jnp.dot(a, b, preferred_element_type=F32)


def _dot_nt(a, b):
    return lax.dot_general(a, b, (((1,), (1,)), ((), ())), preferred_element_type=F32)


def _dot_tn(a, b):
    return lax.dot_general(a, b, (((0,), (0,)), ((), ())), preferred_element_type=F32)


def _split3(x):
    hi = x.astype(BF16)
    r = x - hi.astype(F32)
    mid = r.astype(BF16)
    lo = (r - mid.astype(F32)).astype(BF16)
    return hi, mid, lo


def _dot_exact_rhs(a_bf16, x):
    hi, mid, lo = _split3(x)
    return _dot(a_bf16, hi) + (_dot(a_bf16, mid) + _dot(a_bf16, lo))


def _dot_exact_lhs(x, a_bf16):
    hi, mid, lo = _split3(x)
    return _dot(hi, a_bf16) + (_dot(mid, a_bf16) + _dot(lo, a_bf16))


def _layernorm(y, g, b):
    mu = jnp.mean(y, axis=-1, keepdims=True)
    yc = y - mu
    var = jnp.mean(yc * yc, axis=-1, keepdims=True)
    return yc * lax.rsqrt(var + LN_EPS) * g + b


def _softplus(x):
    return jnp.maximum(x, 0.0) + jnp.log(1.0 + jnp.exp(-jnp.abs(x)))


def _ffn_ln_kernel(x_ref, win_ref, wout_ref, g_ref, b_ref, o_ref):
    x = x_ref[...]
    xb = x.astype(BF16)
    acc = jnp.zeros(x.shape, F32)
    for c in range(D_FF // FFN_CHUNK):
        lo = c * FFN_CHUNK
        gate = _dot(xb, win_ref[:, lo:lo + FFN_CHUNK])
        up = _dot(xb, win_ref[:, D_FF + lo:D_FF + lo + FFN_CHUNK])
        a = (gate * jax.nn.sigmoid(gate) * up).astype(BF16)
        acc = acc + _dot(a, wout_ref[lo:lo + FFN_CHUNK, :])
    y = DEEPNORM_ALPHA * x + FFN_RES * acc
    o_ref[...] = _layernorm(y, g_ref[...], b_ref[...])


def _ffn_ln(x, w_in, w_out, g, b):
    t = x.shape[0]
    tm = min(FFN_TM, t)
    const = lambda i: (0, 0)
    return pl.pallas_call(
        _ffn_ln_kernel,
        grid=(t // tm,),
        in_specs=[
            pl.BlockSpec((tm, D_MODEL), lambda i: (i, 0)),
            pl.BlockSpec((D_MODEL, 2 * D_FF), const, pipeline_mode=pl.Buffered(1)),
            pl.BlockSpec((D_FF, D_MODEL), const, pipeline_mode=pl.Buffered(1)),
            pl.BlockSpec((1, D_MODEL), const),
            pl.BlockSpec((1, D_MODEL), const),
        ],
        out_specs=pl.BlockSpec((tm, D_MODEL), lambda i: (i, 0)),
        out_shape=jax.ShapeDtypeStruct((t, D_MODEL), F32),
        compiler_params=_cparams(("parallel",)),
        name="ffn_ln",
    )(x, w_in, w_out, g.reshape(1, D_MODEL), b.reshape(1, D_MODEL))


def _proj_kernel(x_ref, w_ref, o_ref):
    o_ref[...] = _dot(x_ref[...].astype(BF16), w_ref[...]).astype(o_ref.dtype)


def _proj(x, w, out_dtype, tn):
    t = x.shape[0]
    n = w.shape[1]
    tm = min(PROJ_TM, t)
    return pl.pallas_call(
        _proj_kernel,
        grid=(t // tm, n // tn),
        in_specs=[
            pl.BlockSpec((tm, D_MODEL), lambda i, j: (i, 0)),
            pl.BlockSpec((D_MODEL, tn), lambda i, j: (0, j)),
        ],
        out_specs=pl.BlockSpec((tm, tn), lambda i, j: (i, j)),
        out_shape=jax.ShapeDtypeStruct((t, n), out_dtype),
        compiler_params=_cparams(("parallel", "parallel")),
        name="proj",
    )(x, w)


def _proj_rope_kernel(x_ref, w_ref, c_ref, s1_ref, s2_ref, o_ref):
    acc = _dot(x_ref[...].astype(BF16), w_ref[...])
    c = c_ref[...]
    s1 = s1_ref[...]
    s2 = s2_ref[...]
    for h in range(acc.shape[1] // LANES):
        a = acc[:, h * LANES:(h + 1) * LANES]
        r = a * c + pltpu.roll(a, 8, axis=1) * s1 + pltpu.roll(a, LANES - 8, axis=1) * s2
        o_ref[:, h * LANES:(h + 1) * LANES] = r.astype(o_ref.dtype)


def _proj_rope(x, w, tabs, seq, tn):
    t = x.shape[0]
    n = w.shape[1]
    tm = min(PROJ_TM, seq)
    nrow = seq // tm
    tab_spec = pl.BlockSpec((tm, LANES), lambda i, j: (i % nrow, 0))
    return pl.pallas_call(
        _proj_rope_kernel,
        grid=(t // tm, n // tn),
        in_specs=[
            pl.BlockSpec((tm, D_MODEL), lambda i, j: (i, 0)),
            pl.BlockSpec((D_MODEL, tn), lambda i, j: (0, j)),
            tab_spec, tab_spec, tab_spec,
        ],
        out_specs=pl.BlockSpec((tm, tn), lambda i, j: (i, j)),
        out_shape=jax.ShapeDtypeStruct((t, n), BF16),
        compiler_params=_cparams(("parallel", "parallel")),
        name="proj_rope",
    )(x, w, *tabs)


def _rope_tables(seq):
    half = ROT_A // 2
    inv_freq = ROPE_THETA ** (-jnp.arange(half, dtype=F32) * 2.0 / ROT_A)
    ang = jnp.arange(seq, dtype=F32)[:, None] * inv_freq[None, :]
    cos, sin = jnp.cos(ang), jnp.sin(ang)
    ones = jnp.ones((seq, HD_A - ROT_A), F32)
    zeros = jnp.zeros((seq, HD_A - ROT_A), F32)
    zh = jnp.zeros((seq, half), F32)
    c = jnp.concatenate([cos, cos, ones], axis=1)
    s1 = jnp.concatenate([zh, sin, zeros], axis=1)
    s2 = jnp.concatenate([-sin, zh, zeros], axis=1)
    rep = LANES // HD_A
    return tuple(jnp.tile(v, (1, rep)) for v in (c, s1, s2))


def _diff_attn_kernel(q_ref, k_ref, v_ref, lam_ref, g_ref, o_ref, *, lam_init):
    t = q_ref.shape[0]
    i = pl.program_id(2)
    lane = lax.broadcasted_iota(jnp.int32, (t, LANES), 1)
    q = q_ref[...] * jnp.asarray(HD_A ** -0.5, BF16)
    zero = jnp.zeros_like(q)
    qq = jnp.concatenate([jnp.where(lane < HD_A, q, zero), jnp.where(lane < HD_A, zero, q)], axis=0)

    def step(j, n, carry, masked):
        m, l, acc = carry
        start = pl.multiple_of(j * t, t)
        kb = k_ref[pl.ds(start, n * t), :]
        vb = v_ref[pl.ds(start, n * t), :]
        s = _dot_nt(qq, kb)
        if masked:
            qchunk = lax.broadcasted_iota(jnp.int32, (2 * t, t), 0) % t // CHUNK
            kchunk = lax.broadcasted_iota(jnp.int32, (2 * t, t), 1) // CHUNK
            s = jnp.where(kchunk <= qchunk, s, NEG_BIG)
        m_new = jnp.maximum(m, jnp.max(s, axis=1, keepdims=True))
        p = jnp.exp(s - m_new)
        corr = jnp.exp(m - m_new)
        l = corr * l + jnp.sum(p, axis=1, keepdims=True)
        acc = corr * acc + _dot(p.astype(BF16), vb)
        return m_new, l, acc

    carry = (jnp.full((2 * t, 1), NEG_BIG, F32), jnp.zeros((2 * t, 1), F32),
             jnp.zeros((2 * t, LANES), F32))
    done = 0
    n = ATT_A_NT
    while n >= 1:
        cnt = (i - done) // n
        carry = lax.fori_loop(0, cnt, lambda g, c, n=n, done=done: step(done + g * n, n, c, False), carry)
        done = done + cnt * n
        n //= 2
    m, l, acc = step(i, 1, carry, True)
    o = acc / l
    lf = lam_ref[...]
    lam_full = (jnp.exp(jnp.sum(lf[0:1] * lf[1:2], axis=1, keepdims=True))
                - jnp.exp(jnp.sum(lf[2:3] * lf[3:4], axis=1, keepdims=True)) + lam_init)
    d = o[:t] - lam_full * o[t:]
    y = d * lax.rsqrt(jnp.mean(d * d, axis=-1, keepdims=True) + LN_EPS) * g_ref[...]
    o_ref[...] = (y * (1.0 - lam_init)).astype(o_ref.dtype)


def _diff_attn(qk, v_arr, v_col0, lam, subln_g, layer, batch, seq):
    t = min(ATT_A_T, seq)
    lam_init = 0.8 - 0.6 * math.exp(-0.3 * layer)
    kern = functools.partial(_diff_attn_kernel, lam_init=lam_init)
    return pl.pallas_call(
        kern,
        grid=(batch, H_A, seq // t),
        in_specs=[
            pl.BlockSpec((None, t, LANES), lambda b, h, i: (b, i, h)),
            pl.BlockSpec((None, seq, LANES), lambda b, h, i: (b, 0, H_A + h)),
            pl.BlockSpec((None, seq, LANES), lambda b, h, i: (b, 0, v_col0 + h)),
            pl.BlockSpec((4, HD_A), lambda b, h, i: (0, 0)),
            pl.BlockSpec((1, 2 * HD_A), lambda b, h, i: (0, 0)),
        ],
        out_specs=pl.BlockSpec((None, t, LANES), lambda b, h, i: (b, i, h)),
        out_shape=jax.ShapeDtypeStruct((batch, seq, A_W), BF16),
        compiler_params=_cparams(("parallel", "parallel", "parallel")),
        name="diff_attn",
    )(qk, qk, v_arr, lam, subln_g.reshape(1, 2 * HD_A))


def _stick_kernel(q_ref, k_ref, v_ref, o_ref):
    tq = q_ref.shape[0]
    tk = min(ATT_C_TK, tq)
    per = tq // tk
    i = pl.program_id(2)
    lane = lax.broadcasted_iota(jnp.int32, (tq, LANES), 1)
    q = q_ref[...] * jnp.asarray(HD_C ** -0.5, BF16)
    zero = jnp.zeros_like(q)
    qq = jnp.concatenate([jnp.where(lane < HD_C, q, zero), jnp.where(lane < HD_C, zero, q)], axis=0)
    u = jnp.where(lax.broadcasted_iota(jnp.int32, (tk, tk), 0) > lax.broadcasted_iota(jnp.int32, (tk, tk), 1),
                  -1.0, 0.0).astype(BF16)

    def strict_mask(d):
        return (lax.broadcasted_iota(jnp.int32, (2 * tq, tk), 1) + d * tk
                < lax.broadcasted_iota(jnp.int32, (2 * tq, tk), 0) % tq)

    def front(j, d):
        kb = k_ref[pl.ds(pl.multiple_of(j * tk, tk), tk), :]
        z = _dot_nt(qq, kb) * LOG2E
        neg_abs = lax.bitcast_convert_type(
            lax.bitcast_convert_type(z, jnp.uint32) | jnp.uint32(0x80000000), F32)
        sp = jnp.maximum(z, 0.0) + jnp.log2(1.0 + jnp.exp2(neg_abs))
        sp_m = sp if d is None else jnp.where(strict_mask(d), sp, 0.0)
        hi = sp_m.astype(BF16)
        lo = (sp_m - hi.astype(F32)).astype(BF16)
        after = _dot(hi, u) + _dot(lo, u)
        return (z - sp) + after, jnp.sum(sp_m, axis=1, keepdims=True)

    def back(j, w, run, d):
        a = jnp.exp2(w + run)
        if d is not None:
            a = jnp.where(strict_mask(d), a, 0.0)
        return _dot(a.astype(BF16), v_ref[pl.ds(pl.multiple_of(j * tk, tk), tk), :])

    def group(js, ds, carry):
        run, acc = carry
        fronts = [front(j, d) for j, d in zip(js, ds)]
        for j, d, (w, mass) in zip(js, ds, fronts):
            acc = acc + back(j, w, run, d)
            run = run - mass
        return run, acc

    def full_group(first, n, carry):
        return group([first - m for m in range(n)], [None] * n, carry)

    init = (jnp.zeros((2 * tq, 1), F32), jnp.zeros((2 * tq, LANES), F32))
    own = list(range(per - 1, -1, -1))
    carry = group([per * i + d for d in own], own, init)
    nfull = per * i
    ngrp = nfull // ATT_C_NT
    carry = lax.fori_loop(0, ngrp, lambda n, c: full_group(nfull - 1 - ATT_C_NT * n, ATT_C_NT, c), carry)
    nrest = (nfull - ATT_C_NT * ngrp) // per
    _, acc = lax.fori_loop(0, nrest, lambda n, c: full_group(per * (nrest - n) - 1, per, c), carry)
    o_ref[...] = jnp.where(lane < HD_C, acc[:tq], acc[tq:]).astype(o_ref.dtype)


def _stick_breaking(arr, q_col0, k_col0, v_col0, batch, seq):
    t = min(ATT_C_T, seq)
    nhp = C_W // LANES
    return pl.pallas_call(
        _stick_kernel,
        grid=(batch, nhp, seq // t),
        in_specs=[
            pl.BlockSpec((None, t, LANES), lambda b, h, i: (b, i, q_col0 + h)),
            pl.BlockSpec((None, seq, LANES), lambda b, h, i: (b, 0, k_col0 + h)),
            pl.BlockSpec((None, seq, LANES), lambda b, h, i: (b, 0, v_col0 + h)),
        ],
        out_specs=pl.BlockSpec((None, t, LANES), lambda b, h, i: (b, i, h)),
        out_shape=jax.ShapeDtypeStruct((batch, seq, C_W), BF16),
        compiler_params=_cparams(("parallel", "parallel", "parallel")),
        name="stick_breaking",
    )(arr, arr, arr)


HALO = 8


def _ssd_kernel(z_ref, x_ref, bc_ref, dt_ref, cw_ref, cb_ref, dtb_ref, alog_ref, dskip_ref, ng_ref,
                o_ref, xe_ref, st_ref):
    L = z_ref.shape[0]
    c = pl.program_id(1)

    @pl.when(c == 0)
    def _():
        xe_ref[0:HALO, :] = jnp.zeros((HALO, CONV_DIM), F32)
        st_ref[...] = jnp.zeros(st_ref.shape, F32)

    xe_ref[HALO:HALO + L, 0:D_INNER] = x_ref[...]
    xe_ref[HALO:HALO + L, D_INNER:CONV_DIM] = bc_ref[...]
    conv = cb_ref[...]
    for r in range(CONV_B):
        off = HALO - (CONV_B - 1) + r
        conv = conv + cw_ref[r:r + 1, :] * xe_ref[off:off + L, :]
    xe_ref[0:HALO, :] = xe_ref[L:L + HALO, :]
    xc = conv * jax.nn.sigmoid(conv)
    xs = xc[:, :D_INNER]

    dt = _softplus(dt_ref[...] + dtb_ref[...])
    a = dt * (-jnp.exp(alog_ref[...]))
    row_i = lax.broadcasted_iota(jnp.int32, (L, L), 0)
    col_i = lax.broadcasted_iota(jnp.int32, (L, L), 1)
    causal = col_i <= row_i
    tri = jnp.where(causal, 1.0, 0.0).astype(BF16)
    acum = _dot_exact_rhs(tri, a)
    acum_t = acum.T
    a_last = acum[L - 1:L, :]
    e_in = jnp.exp(acum)
    e_out = jnp.exp(a_last - acum)

    expand = jnp.where(lax.broadcasted_iota(jnp.int32, (LANES, D_INNER), 0)
                       == lax.broadcasted_iota(jnp.int32, (LANES, D_INNER), 1) // P_B,
                       1.0, 0.0).astype(BF16)
    dt_x = _dot_exact_lhs(dt, expand)
    e_in_x = _dot_exact_lhs(e_in, expand)
    e_out_x = _dot_exact_lhs(e_out, expand)

    xdt = xs * dt_x
    xdt_b = xdt.astype(BF16)
    xdt_out_b = (xdt * e_out_x).astype(BF16)
    lane = lax.broadcasted_iota(jnp.int32, (L, LANES), 1)

    hg = H_B // G_B
    gw = D_INNER // G_B
    y_parts = []
    for g in range(G_B):
        bm_f = xc[:, D_INNER + g * N_B:D_INNER + (g + 1) * N_B]
        bm = bm_f.astype(BF16)
        bm_t = bm_f.T.astype(BF16)
        cm = xc[:, D_INNER + G_B * N_B + g * N_B:D_INNER + G_B * N_B + (g + 1) * N_B].astype(BF16)
        cb = _dot_nt(cm, bm)
        st_g = st_ref[:, g * gw:(g + 1) * gw]
        y_off = _dot(cm, st_g.astype(BF16)) * e_in_x[:, g * gw:(g + 1) * gw]
        y_diag = []
        for hp in range(hg // 2):
            h0 = g * hg + 2 * hp
            pair = xdt_b[:, h0 * P_B:(h0 + 2) * P_B]
            ys = []
            for h in (h0, h0 + 1):
                seg = acum[:, h:h + 1] - acum_t[h:h + 1, :]
                m = (cb * jnp.exp(jnp.where(causal, seg, -jnp.inf))).astype(BF16)
                ys.append(_dot(m, pair))
            y_diag.append(jnp.where(lane < P_B, ys[0], ys[1]))
        y_parts.append(jnp.concatenate(y_diag, axis=1) + y_off)
        upd = _dot(bm_t, xdt_out_b[:, g * gw:(g + 1) * gw])
        st_ref[:, g * gw:(g + 1) * gw] = st_g * e_in_x[L - 1:L, g * gw:(g + 1) * gw] + upd

    y = jnp.concatenate(y_parts, axis=1) + dskip_ref[...] * xs
    zz = z_ref[...]
    y = y * (zz * jax.nn.sigmoid(zz))
    outs = []
    for g in range(G_B):
        yg = y[:, g * gw:(g + 1) * gw]
        outs.append(yg * lax.rsqrt(jnp.mean(yg * yg, axis=-1, keepdims=True) + LN_EPS))
    o_ref[...] = (jnp.concatenate(outs, axis=1) * ng_ref[...]).astype(o_ref.dtype)


def _ssd(ssm, conv_w, conv_b, dt_bias, a_log, d_skip, norm_g, batch, seq):
    L = min(SSD_L, seq)
    pad = LANES - H_B
    row = lambda v: v.reshape(1, -1).astype(F32)
    dtb = jnp.pad(row(dt_bias), ((0, 0), (0, pad)))
    alog = jnp.pad(row(a_log), ((0, 0), (0, pad)))
    dskip = jnp.repeat(row(d_skip), P_B, axis=1)
    const = lambda b, c: (0, 0)
    bc_w = CONV_DIM - D_INNER
    return pl.pallas_call(
        _ssd_kernel,
        grid=(batch, seq // L),
        in_specs=[
            pl.BlockSpec((None, L, D_INNER), lambda b, c: (b, c, 0)),
            pl.BlockSpec((None, L, D_INNER), lambda b, c: (b, c, 1)),
            pl.BlockSpec((None, L, bc_w), lambda b, c: (b, c, 2 * D_INNER // bc_w)),
            pl.BlockSpec((None, L, LANES), lambda b, c: (b, c, (D_INNER + CONV_DIM) // LANES)),
            pl.BlockSpec((CONV_B, CONV_DIM), const),
            pl.BlockSpec((1, CONV_DIM), const),
            pl.BlockSpec((1, LANES), const),
            pl.BlockSpec((1, LANES), const),
            pl.BlockSpec((1, D_INNER), const),
            pl.BlockSpec((1, D_INNER), const),
        ],
        out_specs=pl.BlockSpec((None, L, D_INNER), lambda b, c: (b, c, 0)),
        out_shape=jax.ShapeDtypeStruct((batch, seq, D_INNER), BF16),
        scratch_shapes=[pltpu.VMEM((HALO + L, CONV_DIM), F32), pltpu.VMEM((N_B, D_INNER), F32)],
        compiler_params=_cparams(("parallel", "arbitrary")),
        name="ssd",
    )(ssm, ssm, ssm, ssm, conv_w, row(conv_b), dtb, alog, dskip, row(norm_g))


def _merge_ln_kernel(h_ref, oa_ref, ob_ref, oc_ref, wg_ref, gb_ref, wbr_ref, wout_ref, g_ref, b_ref, o_ref):
    h = h_ref[...]
    hb = h.astype(BF16)
    merged = jnp.zeros(h.shape, F32)
    for r, br_ref in enumerate((oa_ref, ob_ref, oc_ref)):
        gpre = _dot(hb, wg_ref[:, r * D_MODEL:(r + 1) * D_MODEL]) + gb_ref[r:r + 1, :]
        merged = merged + jax.nn.sigmoid(gpre) * _dot(br_ref[...], wbr_ref[r])
    mix = _dot(merged.astype(BF16), wout_ref[...])
    o_ref[...] = _layernorm(DEEPNORM_ALPHA * h + mix, g_ref[...], b_ref[...])


def _merge_ln(h, oa, ob, oc, w_gate, gate_bias, w_branch, w_out, g, b):
    t = h.shape[0]
    tm = min(MERGE_TM, t)
    rowspec = pl.BlockSpec((tm, D_MODEL), lambda i: (i, 0))
    const2 = lambda i: (0, 0)
    one = pl.Buffered(1)
    return pl.pallas_call(
        _merge_ln_kernel,
        grid=(t // tm,),
        in_specs=[
            rowspec, rowspec, rowspec, rowspec,
            pl.BlockSpec((D_MODEL, N_BRANCH * D_MODEL), const2, pipeline_mode=one),
            pl.BlockSpec((N_BRANCH, D_MODEL), const2),
            pl.BlockSpec((N_BRANCH, D_MODEL, D_MODEL), lambda i: (0, 0, 0), pipeline_mode=one),
            pl.BlockSpec((D_MODEL, D_MODEL), const2, pipeline_mode=one),
            pl.BlockSpec((1, D_MODEL), const2),
            pl.BlockSpec((1, D_MODEL), const2),
        ],
        out_specs=rowspec,
        out_shape=jax.ShapeDtypeStruct((t, D_MODEL), F32),
        compiler_params=_cparams(("parallel",)),
        name="merge_ln",
    )(h, oa, ob, oc, w_gate, gate_bias, w_branch, w_out, g.reshape(1, D_MODEL), b.reshape(1, D_MODEL))


def _mixer_weights(w_in):
    segs = (A_W, A_W, A_W, D_INNER, CONV_DIM, H_B, C_W, C_W, C_W, N_BRANCH * D_MODEL)
    offs = [0]
    for s in segs:
        offs.append(offs[-1] + s)
    qa, ka, va, zb, xbc, dtb, qc, kc, vc, gate = (w_in[:, offs[n]:offs[n + 1]] for n in range(len(segs)))
    w_qk = jnp.concatenate([qa, ka], axis=1).astype(BF16)
    w_att = jnp.concatenate([va, qc, kc, vc], axis=1).astype(BF16)
    w_ssm = jnp.concatenate([zb, xbc, jnp.pad(dtb, ((0, 0), (0, LANES - H_B)))], axis=1).astype(BF16)
    return w_qk, w_att, w_ssm, gate.astype(BF16)


def kernel(x, ffn1_w_in, ffn1_w_out, ln1_g, ln1_b, w_mix_in, gate_bias, diff_lambda, diff_subln_g,
           ssm_conv_w, ssm_conv_b, ssm_dt_bias, ssm_A_log, ssm_D, ssm_norm_g, w_branch, w_mix_out,
           ln2_g, ln2_b, ffn2_w_in, ffn2_w_out, ln3_g, ln3_b):
    batch, seq, _ = x.shape
    t = batch * seq
    tabs = _rope_tables(seq)
    xf = x.reshape(t, D_MODEL)
    for l in range(DEPTH):
        xf = _ffn_ln(xf, ffn1_w_in[l].astype(BF16), ffn1_w_out[l].astype(BF16), ln1_g[l], ln1_b[l])

        w_qk, w_att, w_ssm, w_gate = _mixer_weights(w_mix_in[l])
        qk = _proj_rope(xf, w_qk, tabs, seq, PROJ_TN).reshape(batch, seq, 2 * A_W)
        att = _proj(xf, w_att, BF16, PROJ_TN).reshape(batch, seq, A_W + 3 * C_W)
        ssm = _proj(xf, w_ssm, F32, w_ssm.shape[1] // 3).reshape(batch, seq, w_ssm.shape[1])

        o_a = _diff_attn(qk, att, 0, diff_lambda[l], diff_subln_g[l], l, batch, seq)
        nb = A_W // LANES
        o_c = _stick_breaking(att, nb, 2 * nb, 3 * nb, batch, seq)
        o_b = _ssd(ssm, ssm_conv_w[l], ssm_conv_b[l], ssm_dt_bias[l], ssm_A_log[l], ssm_D[l],
                   ssm_norm_g[l], batch, seq)

        xf = _merge_ln(xf, o_a.reshape(t, A_W), o_b.reshape(t, D_INNER), o_c.reshape(t, C_W),
                       w_gate, gate_bias[l], w_branch[l].astype(BF16), w_mix_out[l].astype(BF16),
                       ln2_g[l], ln2_b[l])

        xf = _ffn_ln(xf, ffn2_w_in[l].astype(BF16), ffn2_w_out[l].astype(BF16), ln3_g[l], ln3_b[l])
    return xf.reshape(batch, seq, D_MODEL)
```

```python
import functools
import math

import jax
import jax.numpy as jnp
from jax import lax
from jax.experimental import pallas as pl
from jax.experimental.pallas import tpu as pltpu

F32 = jnp.float32
BF16 = jnp.bfloat16

D_MODEL = 1024
DEPTH = 2
LN_EPS = 1e-5
DEEPNORM_ALPHA = (2 * DEPTH) ** 0.25
D_FF = 2816
FFN_RES = 0.5

CHUNK = 64
H_A = 8
HD_A = 64
ROT_A = HD_A // 4
ROPE_THETA = 500000.0
A_W = H_A * 2 * HD_A

H_B = 16
P_B = 64
D_INNER = H_B * P_B
G_B = 2
N_B = 128
CONV_B = 4
CONV_DIM = D_INNER + 2 * G_B * N_B

H_C = 16
HD_C = 64
C_W = H_C * HD_C
N_BRANCH = 3

LANES = 128
VMEM_LIMIT = 56 * 1024 * 1024

FFN_TM = 512
FFN_CHUNK = 256
PROJ_TM = 512
PROJ_TN = 1024
MERGE_TM = 512
ATT_A_T = 512
ATT_A_NT = 4
ATT_C_T = 512
ATT_C_TK = 256
ATT_C_NT = 4
SSD_L = 256

NEG_BIG = -1e30
LOG2E = 1.4426950408889634


def _cparams(sem):
    return pltpu.CompilerParams(dimension_semantics=sem, vmem_limit_bytes=VMEM_LIMIT)


def _dot(a, b):
    return jnp.dot(a, b, preferred_element_type=F32)


def _dot_nt(a, b):
    return lax.dot_general(a, b, (((1,), (1,)), ((), ())), preferred_element_type=F32)


def _dot_tn(a, b):
    return lax.dot_general(a, b, (((0,), (0,)), ((), ())), preferred_element_type=F32)


def _split3(x):
    hi = x.astype(BF16)
    r = x - hi.astype(F32)
    mid = r.astype(BF16)
    lo = (r - mid.astype(F32)).astype(BF16)
    return hi, mid, lo


def _dot_exact_rhs(a_bf16, x):
    hi, mid, lo = _split3(x)
    return _dot(a_bf16, hi) + (_dot(a_bf16, mid) + _dot(a_bf16, lo))


def _dot_exact_lhs(x, a_bf16):
    hi, mid, lo = _split3(x)
    return _dot(hi, a_bf16) + (_dot(mid, a_bf16) + _dot(lo, a_bf16))


def _layernorm(y, g, b):
    mu = jnp.mean(y, axis=-1, keepdims=True)
    yc = y - mu
    var = jnp.mean(yc * yc, axis=-1, keepdims=True)
    return yc * lax.rsqrt(var + LN_EPS) * g + b


def _softplus(x):
    return jnp.maximum(x, 0.0) + jnp.log(1.0 + jnp.exp(-jnp.abs(x)))


def _ffn_ln_kernel(x_ref, win_ref, wout_ref, g_ref, b_ref, o_ref):
    x = x_ref[...]
    xb = x.astype(BF16)
    acc = jnp.zeros(x.shape, F32)
    for c in range(D_FF // FFN_CHUNK):
        lo = c * FFN_CHUNK
        gate = _dot(xb, win_ref[:, lo:lo + FFN_CHUNK])
        up = _dot(xb, win_ref[:, D_FF + lo:D_FF + lo + FFN_CHUNK])
        a = (gate * jax.nn.sigmoid(gate) * up).astype(BF16)
        acc = acc + _dot(a, wout_ref[lo:lo + FFN_CHUNK, :])
    y = DEEPNORM_ALPHA * x + FFN_RES * acc
    o_ref[...] = _layernorm(y, g_ref[...], b_ref[...])


def _ffn_ln(x, w_in, w_out, g, b):
    t = x.shape[0]
    tm = min(FFN_TM, t)
    const = lambda i: (0, 0)
    return pl.pallas_call(
        _ffn_ln_kernel,
        grid=(t // tm,),
        in_specs=[
            pl.BlockSpec((tm, D_MODEL), lambda i: (i, 0)),
            pl.BlockSpec((D_MODEL, 2 * D_FF), const, pipeline_mode=pl.Buffered(1)),
            pl.BlockSpec((D_FF, D_MODEL), const, pipeline_mode=pl.Buffered(1)),
            pl.BlockSpec((1, D_MODEL), const),
            pl.BlockSpec((1, D_MODEL), const),
        ],
        out_specs=pl.BlockSpec((tm, D_MODEL), lambda i: (i, 0)),
        out_shape=jax.ShapeDtypeStruct((t, D_MODEL), F32),
        compiler_params=_cparams(("parallel",)),
        name="ffn_ln",
    )(x, w_in, w_out, g.reshape(1, D_MODEL), b.reshape(1, D_MODEL))


def _proj_kernel(x_ref, w_ref, o_ref):
    o_ref[...] = _dot(x_ref[...].astype(BF16), w_ref[...]).astype(o_ref.dtype)


def _proj(x, w, out_dtype, tn):
    t = x.shape[0]
    n = w.shape[1]
    tm = min(PROJ_TM, t)
    return pl.pallas_call(
        _proj_kernel,
        grid=(t // tm, n // tn),
        in_specs=[
            pl.BlockSpec((tm, D_MODEL), lambda i, j: (i, 0)),
            pl.BlockSpec((D_MODEL, tn), lambda i, j: (0, j)),
        ],
        out_specs=pl.BlockSpec((tm, tn), lambda i, j: (i, j)),
        out_shape=jax.ShapeDtypeStruct((t, n), out_dtype),
        compiler_params=_cparams(("parallel", "parallel")),
        name="proj",
    )(x, w)


def _proj_rope_kernel(x_ref, w_ref, c_ref, s1_ref, s2_ref, o_ref):
    acc = _dot(x_ref[...].astype(BF16), w_ref[...])
    c = c_ref[...]
    s1 = s1_ref[...]
    s2 = s2_ref[...]
    for h in range(acc.shape[1] // LANES):
        a = acc[:, h * LANES:(h + 1) * LANES]
        r = a * c + pltpu.roll(a, 8, axis=1) * s1 + pltpu.roll(a, LANES - 8, axis=1) * s2
        o_ref[:, h * LANES:(h + 1) * LANES] = r.astype(o_ref.dtype)


def _proj_rope(x, w, tabs, seq, tn):
    t = x.shape[0]
    n = w.shape[1]
    tm = min(PROJ_TM, seq)
    nrow = seq // tm
    tab_spec = pl.BlockSpec((tm, LANES), lambda i, j: (i % nrow, 0))
    return pl.pallas_call(
        _proj_rope_kernel,
        grid=(t // tm, n // tn),
        in_specs=[
            pl.BlockSpec((tm, D_MODEL), lambda i, j: (i, 0)),
            pl.BlockSpec((D_MODEL, tn), lambda i, j: (0, j)),
            tab_spec, tab_spec, tab_spec,
        ],
        out_specs=pl.BlockSpec((tm, tn), lambda i, j: (i, j)),
        out_shape=jax.ShapeDtypeStruct((t, n), BF16),
        compiler_params=_cparams(("parallel", "parallel")),
        name="proj_rope",
    )(x, w, *tabs)


def _rope_tables(seq):
    half = ROT_A // 2
    inv_freq = ROPE_THETA ** (-jnp.arange(half, dtype=F32) * 2.0 / ROT_A)
    ang = jnp.arange(seq, dtype=F32)[:, None] * inv_freq[None, :]
    cos, sin = jnp.cos(ang), jnp.sin(ang)
    ones = jnp.ones((seq, HD_A - ROT_A), F32)
    zeros = jnp.zeros((seq, HD_A - ROT_A), F32)
    zh = jnp.zeros((seq, half), F32)
    c = jnp.concatenate([cos, cos, ones], axis=1)
    s1 = jnp.concatenate([zh, sin, zeros], axis=1)
    s2 = jnp.concatenate([-sin, zh, zeros], axis=1)
    rep = LANES // HD_A
    return tuple(jnp.tile(v, (1, rep)) for v in (c, s1, s2))


def _diff_attn_kernel(q_ref, k_ref, v_ref, lam_ref, g_ref, o_ref, *, lam_init):
    t = q_ref.shape[0]
    i = pl.program_id(2)
    lane = lax.broadcasted_iota(jnp.int32, (t, LANES), 1)
    q = q_ref[...] * jnp.asarray(HD_A ** -0.5, BF16)
    zero = jnp.zeros_like(q)
    qq = jnp.concatenate([jnp.where(lane < HD_A, q, zero), jnp.where(lane < HD_A, zero, q)], axis=0)

    def step(j, n, carry, masked):
        m, l, acc = carry
        start = pl.multiple_of(j * t, t)
        kb = k_ref[pl.ds(start, n * t), :]
        vb = v_ref[pl.ds(start, n * t), :]
        s = _dot_nt(qq, kb)
        if masked:
            qchunk = lax.broadcasted_iota(jnp.int32, (2 * t, t), 0) % t // CHUNK
            kchunk = lax.broadcasted_iota(jnp.int32, (2 * t, t), 1) // CHUNK
            s = jnp.where(kchunk <= qchunk, s, NEG_BIG)
        m_new = jnp.maximum(m, jnp.max(s, axis=1, keepdims=True))
        p = jnp.exp(s - m_new)
        corr = jnp.exp(m - m_new)
        l = corr * l + jnp.sum(p, axis=1, keepdims=True)
        acc = corr * acc + _dot(p.astype(BF16), vb)
        return m_new, l, acc

    carry = (jnp.full((2 * t, 1), NEG_BIG, F32), jnp.zeros((2 * t, 1), F32),
             jnp.zeros((2 * t, LANES), F32))
    done = 0
    n = ATT_A_NT
    while n >= 1:
        cnt = (i - done) // n
        carry = lax.fori_loop(0, cnt, lambda g, c, n=n, done=done: step(done + g * n, n, c, False), carry)
        done = done + cnt * n
        n //= 2
    m, l, acc = step(i, 1, carry, True)
    o = acc / l
    lf = lam_ref[...]
    lam_full = (jnp.exp(jnp.sum(lf[0:1] * lf[1:2], axis=1, keepdims=True))
                - jnp.exp(jnp.sum(lf[2:3] * lf[3:4], axis=1, keepdims=True)) + lam_init)
    d = o[:t] - lam_full * o[t:]
    y = d * lax.rsqrt(jnp.mean(d * d, axis=-1, keepdims=True) + LN_EPS) * g_ref[...]
    o_ref[...] = (y * (1.0 - lam_init)).astype(o_ref.dtype)


def _diff_attn(qk, v_arr, v_col0, lam, subln_g, layer, batch, seq):
    t = min(ATT_A_T, seq)
    lam_init = 0.8 - 0.6 * math.exp(-0.3 * layer)
    kern = functools.partial(_diff_attn_kernel, lam_init=lam_init)
    return pl.pallas_call(
        kern,
        grid=(batch, H_A, seq // t),
        in_specs=[
            pl.BlockSpec((None, t, LANES), lambda b, h, i: (b, i, h)),
            pl.BlockSpec((None, seq, LANES), lambda b, h, i: (b, 0, H_A + h)),
            pl.BlockSpec((None, seq, LANES), lambda b, h, i: (b, 0, v_col0 + h)),
            pl.BlockSpec((4, HD_A), lambda b, h, i: (0, 0)),
            pl.BlockSpec((1, 2 * HD_A), lambda b, h, i: (0, 0)),
        ],
        out_specs=pl.BlockSpec((None, t, LANES), lambda b, h, i: (b, i, h)),
        out_shape=jax.ShapeDtypeStruct((batch, seq, A_W), BF16),
        compiler_params=_cparams(("parallel", "parallel", "parallel")),
        name="diff_attn",
    )(qk, qk, v_arr, lam, subln_g.reshape(1, 2 * HD_A))


def _stick_kernel(q_ref, k_ref, v_ref, o_ref):
    tq = q_ref.shape[0]
    tk = min(ATT_C_TK, tq)
    per = tq // tk
    i = pl.program_id(2)
    lane = lax.broadcasted_iota(jnp.int32, (tq, LANES), 1)
    q = q_ref[...] * jnp.asarray(HD_C ** -0.5, BF16)
    zero = jnp.zeros_like(q)
    qq = jnp.concatenate([jnp.where(lane < HD_C, q, zero), jnp.where(lane < HD_C, zero, q)], axis=0)
    u = jnp.where(lax.broadcasted_iota(jnp.int32, (tk, tk), 0) >= lax.broadcasted_iota(jnp.int32, (tk, tk), 1),
                  -1.0, 0.0).astype(BF16)

    def front(qs, j, tri):
        kb = k_ref[pl.ds(pl.multiple_of(j * tk, tk), tk), :]
        z = _dot_nt(qs, kb) * LOG2E
        neg_abs = lax.bitcast_convert_type(
            lax.bitcast_convert_type(z, jnp.uint32) | jnp.uint32(0x80000000), F32)
        sp = jnp.maximum(z, 0.0) + jnp.log2(1.0 + jnp.exp2(neg_abs))
        if tri is not None:
            sp = jnp.where(tri, sp, 0.0)
        tail = _dot(sp.astype(BF16), u)
        return z + tail, tail[:, 0:1]

    def back(j, w, run, tri):
        a = jnp.exp2(w + run)
        if tri is not None:
            a = jnp.where(tri, a, 0.0)
        return _dot(a.astype(BF16), v_ref[pl.ds(pl.multiple_of(j * tk, tk), tk), :])

    def group(qs, js, tris, carry):
        run, acc = carry
        fronts = [front(qs, j, tri) for j, tri in zip(js, tris)]
        for j, tri, (w, mass) in zip(js, tris, fronts):
            acc = acc + back(j, w, run, tri)
            run = run + mass
        return run, acc

    def full_group(first, n, carry):
        return group(qq, [first - m for m in range(n)], [None] * n, carry)

    tri = (lax.broadcasted_iota(jnp.int32, (2 * tk, tk), 1)
           < lax.broadcasted_iota(jnp.int32, (2 * tk, tk), 0) % tk)
    bands = []
    for r in range(per):
        qs = jnp.concatenate([qq[r * tk:(r + 1) * tk], qq[tq + r * tk:tq + (r + 1) * tk]], axis=0)
        init = (jnp.zeros((2 * tk, 1), F32), jnp.zeros((2 * tk, LANES), F32))
        bands.append(group(qs, [per * i + r - m for m in range(r + 1)], [tri] + [None] * r, init))
    carry = tuple(jnp.concatenate([band[n][:tk] for band in bands] + [band[n][tk:] for band in bands], axis=0)
                  for n in range(2))

    nfull = per * i
    ngrp = nfull // ATT_C_NT
    carry = lax.fori_loop(0, ngrp, lambda n, c: full_group(nfull - 1 - ATT_C_NT * n, ATT_C_NT, c), carry)
    nrest = (nfull - ATT_C_NT * ngrp) // per
    _, acc = lax.fori_loop(0, nrest, lambda n, c: full_group(per * (nrest - n) - 1, per, c), carry)
    o_ref[...] = jnp.where(lane < HD_C, acc[:tq], acc[tq:]).astype(o_ref.dtype)


def _stick_breaking(arr, q_col0, k_col0, v_col0, batch, seq):
    t = min(ATT_C_T, seq)
    nhp = C_W // LANES
    return pl.pallas_call(
        _stick_kernel,
        grid=(batch, nhp, seq // t),
        in_specs=[
            pl.BlockSpec((None, t, LANES), lambda b, h, i: (b, i, q_col0 + h)),
            pl.BlockSpec((None, seq, LANES), lambda b, h, i: (b, 0, k_col0 + h)),
            pl.BlockSpec((None, seq, LANES), lambda b, h, i: (b, 0, v_col0 + h)),
        ],
        out_specs=pl.BlockSpec((None, t, LANES), lambda b, h, i: (b, i, h)),
        out_shape=jax.ShapeDtypeStruct((batch, seq, C_W), BF16),
        compiler_params=_cparams(("parallel", "parallel", "parallel")),
        name="stick_breaking",
    )(arr, arr, arr)


HALO = 8


def _ssd_kernel(z_ref, x_ref, bc_ref, dt_ref, cw_ref, cb_ref, dtb_ref, alog_ref, dskip_ref, ng_ref,
                o_ref, xe_ref, st_ref):
    L = z_ref.shape[0]
    c = pl.program_id(1)

    @pl.when(c == 0)
    def _():
        xe_ref[0:HALO, :] = jnp.zeros((HALO, CONV_DIM), F32)
        st_ref[...] = jnp.zeros(st_ref.shape, F32)

    xe_ref[HALO:HALO + L, 0:D_INNER] = x_ref[...]
    xe_ref[HALO:HALO + L, D_INNER:CONV_DIM] = bc_ref[...]
    conv = cb_ref[...]
    for r in range(CONV_B):
        off = HALO - (CONV_B - 1) + r
        conv = conv + cw_ref[r:r + 1, :] * xe_ref[off:off + L, :]
    xe_ref[0:HALO, :] = xe_ref[L:L + HALO, :]
    xc = conv * jax.nn.sigmoid(conv)
    xs = xc[:, :D_INNER]

    dt = _softplus(dt_ref[...] + dtb_ref[...])
    a = dt * (-jnp.exp(alog_ref[...]))
    row_i = lax.broadcasted_iota(jnp.int32, (L, L), 0)
    col_i = lax.broadcasted_iota(jnp.int32, (L, L), 1)
    causal = col_i <= row_i
    tri = jnp.where(causal, 1.0, 0.0).astype(BF16)
    acum = _dot_exact_rhs(tri, a)
    acum_t = acum.T
    a_last = acum[L - 1:L, :]
    e_in = jnp.exp(acum)
    e_out = jnp.exp(a_last - acum)

    expand = jnp.where(lax.broadcasted_iota(jnp.int32, (LANES, D_INNER), 0)
                       == lax.broadcasted_iota(jnp.int32, (LANES, D_INNER), 1) // P_B,
                       1.0, 0.0).astype(BF16)
    dt_x = _dot_exact_lhs(dt, expand)
    e_in_x = _dot_exact_lhs(e_in, expand)
    e_out_x = _dot_exact_lhs(e_out, expand)

    xdt = xs * dt_x
    xdt_b = xdt.astype(BF16)
    xdt_out_b = (xdt * e_out_x).astype(BF16)
    lane = lax.broadcasted_iota(jnp.int32, (L, LANES), 1)

    hg = H_B // G_B
    gw = D_INNER // G_B
    y_parts = []
    for g in range(G_B):
        bm_f = xc[:, D_INNER + g * N_B:D_INNER + (g + 1) * N_B]
        bm = bm_f.astype(BF16)
        bm_t = bm_f.T.astype(BF16)
        cm = xc[:, D_INNER + G_B * N_B + g * N_B:D_INNER + G_B * N_B + (g + 1) * N_B].astype(BF16)
        cb = _dot_nt(cm, bm)
        st_g = st_ref[:, g * gw:(g + 1) * gw]
        y_off = _dot(cm, st_g.astype(BF16)) * e_in_x[:, g * gw:(g + 1) * gw]
        y_diag = []
        for hp in range(hg // 2):
            h0 = g * hg + 2 * hp
            pair = xdt_b[:, h0 * P_B:(h0 + 2) * P_B]
            ys = []
            for h in (h0, h0 + 1):
                seg = acum[:, h:h + 1] - acum_t[h:h + 1, :]
                m = (cb * jnp.exp(jnp.where(causal, seg, -jnp.inf))).astype(BF16)
                ys.append(_dot(m, pair))
            y_diag.append(jnp.where(lane < P_B, ys[0], ys[1]))
        y_parts.append(jnp.concatenate(y_diag, axis=1) + y_off)
        upd = _dot(bm_t, xdt_out_b[:, g * gw:(g + 1) * gw])
        st_ref[:, g * gw:(g + 1) * gw] = st_g * e_in_x[L - 1:L, g * gw:(g + 1) * gw] + upd

    y = jnp.concatenate(y_parts, axis=1) + dskip_ref[...] * xs
    zz = z_ref[...]
    y = y * (zz * jax.nn.sigmoid(zz))
    outs = []
    for g in range(G_B):
        yg = y[:, g * gw:(g + 1) * gw]
        outs.append(yg * lax.rsqrt(jnp.mean(yg * yg, axis=-1, keepdims=True) + LN_EPS))
    o_ref[...] = (jnp.concatenate(outs, axis=1) * ng_ref[...]).astype(o_ref.dtype)


def _ssd(ssm, conv_w, conv_b, dt_bias, a_log, d_skip, norm_g, batch, seq):
    L = min(SSD_L, seq)
    pad = LANES - H_B
    row = lambda v: v.reshape(1, -1).astype(F32)
    dtb = jnp.pad(row(dt_bias), ((0, 0), (0, pad)))
    alog = jnp.pad(row(a_log), ((0, 0), (0, pad)))
    dskip = jnp.repeat(row(d_skip), P_B, axis=1)
    const = lambda b, c: (0, 0)
    bc_w = CONV_DIM - D_INNER
    return pl.pallas_call(
        _ssd_kernel,
        grid=(batch, seq // L),
        in_specs=[
            pl.BlockSpec((None, L, D_INNER), lambda b, c: (b, c, 0)),
            pl.BlockSpec((None, L, D_INNER), lambda b, c: (b, c, 1)),
            pl.BlockSpec((None, L, bc_w), lambda b, c: (b, c, 2 * D_INNER // bc_w)),
            pl.BlockSpec((None, L, LANES), lambda b, c: (b, c, (D_INNER + CONV_DIM) // LANES)),
            pl.BlockSpec((CONV_B, CONV_DIM), const),
            pl.BlockSpec((1, CONV_DIM), const),
            pl.BlockSpec((1, LANES), const),
            pl.BlockSpec((1, LANES), const),
            pl.BlockSpec((1, D_INNER), const),
            pl.BlockSpec((1, D_INNER), const),
        ],
        out_specs=pl.BlockSpec((None, L, D_INNER), lambda b, c: (b, c, 0)),
        out_shape=jax.ShapeDtypeStruct((batch, seq, D_INNER), BF16),
        scratch_shapes=[pltpu.VMEM((HALO + L, CONV_DIM), F32), pltpu.VMEM((N_B, D_INNER), F32)],
        compiler_params=_cparams(("parallel", "arbitrary")),
        name="ssd",
    )(ssm, ssm, ssm, ssm, conv_w, row(conv_b), dtb, alog, dskip, row(norm_g))


def _merge_ln_kernel(h_ref, oa_ref, ob_ref, oc_ref, wg_ref, gb_ref, wbr_ref, wout_ref, g_ref, b_ref, o_ref):
    h = h_ref[...]
    hb = h.astype(BF16)
    merged = jnp.zeros(h.shape, F32)
    for r, br_ref in enumerate((oa_ref, ob_ref, oc_ref)):
        gpre = _dot(hb, wg_ref[:, r * D_MODEL:(r + 1) * D_MODEL]) + gb_ref[r:r + 1, :]
        merged = merged + jax.nn.sigmoid(gpre) * _dot(br_ref[...], wbr_ref[r])
    mix = _dot(merged.astype(BF16), wout_ref[...])
    o_ref[...] = _layernorm(DEEPNORM_ALPHA * h + mix, g_ref[...], b_ref[...])


def _merge_ln(h, oa, ob, oc, w_gate, gate_bias, w_branch, w_out, g, b):
    t = h.shape[0]
    tm = min(MERGE_TM, t)
    rowspec = pl.BlockSpec((tm, D_MODEL), lambda i: (i, 0))
    const2 = lambda i: (0, 0)
    one = pl.Buffered(1)
    return pl.pallas_call(
        _merge_ln_kernel,
        grid=(t // tm,),
        in_specs=[
            rowspec, rowspec, rowspec, rowspec,
            pl.BlockSpec((D_MODEL, N_BRANCH * D_MODEL), const2, pipeline_mode=one),
            pl.BlockSpec((N_BRANCH, D_MODEL), const2),
            pl.BlockSpec((N_BRANCH, D_MODEL, D_MODEL), lambda i: (0, 0, 0), pipeline_mode=one),
            pl.BlockSpec((D_MODEL, D_MODEL), const2, pipeline_mode=one),
            pl.BlockSpec((1, D_MODEL), const2),
            pl.BlockSpec((1, D_MODEL), const2),
        ],
        out_specs=rowspec,
        out_shape=jax.ShapeDtypeStruct((t, D_MODEL), F32),
        compiler_params=_cparams(("parallel",)),
        name="merge_ln",
    )(h, oa, ob, oc, w_gate, gate_bias, w_branch, w_out, g.reshape(1, D_MODEL), b.reshape(1, D_MODEL))


def _mixer_weights(w_in):
    segs = (A_W, A_W, A_W, D_INNER, CONV_DIM, H_B, C_W, C_W, C_W, N_BRANCH * D_MODEL)
    offs = [0]
    for s in segs:
        offs.append(offs[-1] + s)
    qa, ka, va, zb, xbc, dtb, qc, kc, vc, gate = (w_in[:, offs[n]:offs[n + 1]] for n in range(len(segs)))
    w_qk = jnp.concatenate([qa, ka], axis=1).astype(BF16)
    w_att = jnp.concatenate([va, qc, kc, vc], axis=1).astype(BF16)
    w_ssm = jnp.concatenate([zb, xbc, jnp.pad(dtb, ((0, 0), (0, LANES - H_B)))], axis=1).astype(BF16)
    return w_qk, w_att, w_ssm, gate.astype(BF16)


def kernel(x, ffn1_w_in, ffn1_w_out, ln1_g, ln1_b, w_mix_in, gate_bias, diff_lambda, diff_subln_g,
           ssm_conv_w, ssm_conv_b, ssm_dt_bias, ssm_A_log, ssm_D, ssm_norm_g, w_branch, w_mix_out,
           ln2_g, ln2_b, ffn2_w_in, ffn2_w_out, ln3_g, ln3_b):
    batch, seq, _ = x.shape
    t = batch * seq
    tabs = _rope_tables(seq)
    xf = x.reshape(t, D_MODEL)
    for l in range(DEPTH):
        xf = _ffn_ln(xf, ffn1_w_in[l].astype(BF16), ffn1_w_out[l].astype(BF16), ln1_g[l], ln1_b[l])

        w_qk, w_att, w_ssm, w_gate = _mixer_weights(w_mix_in[l])
        qk = _proj_rope(xf, w_qk, tabs, seq, PROJ_TN).reshape(batch, seq, 2 * A_W)
        att = _proj(xf, w_att, BF16, PROJ_TN).reshape(batch, seq, A_W + 3 * C_W)
        ssm = _proj(xf, w_ssm, F32, w_ssm.shape[1] // 3).reshape(batch, seq, w_ssm.shape[1])

        o_a = _diff_attn(qk, att, 0, diff_lambda[l], diff_subln_g[l], l, batch, seq)
        nb = A_W // LANES
        o_c = _stick_breaking(att, nb, 2 * nb, 3 * nb, batch, seq)
        o_b = _ssd(ssm, ssm_conv_w[l], ssm_conv_b[l], ssm_dt_bias[l], ssm_A_log[l], ssm_D[l],
                   ssm_norm_g[l], batch, seq)

        xf = _merge_ln(xf, o_a.reshape(t, A_W), o_b.reshape(t, D_INNER), o_c.reshape(t, C_W),
                       w_gate, gate_bias[l], w_branch[l].astype(BF16), w_mix_out[l].astype(BF16),
                       ln2_g[l], ln2_b[l])

        xf = _ffn_ln(xf, ffn2_w_in[l].astype(BF16), ffn2_w_out[l].astype(BF16), ln3_g[l], ln3_b[l])
    return xf.reshape(batch, seq, D_MODEL)
```

```python
import functools
import math

import jax
import jax.numpy as jnp
from jax import lax
from jax.experimental import pallas as pl
from jax.experimental.pallas import tpu as pltpu

F32 = jnp.float32
BF16 = jnp.bfloat16

D_MODEL = 1024
DEPTH = 2
LN_EPS = 1e-5
DEEPNORM_ALPHA = (2 * DEPTH) ** 0.25
D_FF = 2816
FFN_RES = 0.5

CHUNK = 64
H_A = 8
HD_A = 64
ROT_A = HD_A // 4
ROPE_THETA = 500000.0
A_W = H_A * 2 * HD_A

H_B = 16
P_B = 64
D_INNER = H_B * P_B
G_B = 2
N_B = 128
CONV_B = 4
CONV_DIM = D_INNER + 2 * G_B * N_B

H_C = 16
HD_C = 64
C_W = H_C * HD_C
N_BRANCH = 3

LANES = 128
VMEM_LIMIT = 56 * 1024 * 1024

FFN_TM = 512
FFN_CHUNK = 256
PROJ_TM = 512
MERGE_TM = 512
ATT_A_T = 512
ATT_A_NT = 4
ATT_C_T = 512
ATT_C_TK = 256
ATT_C_NT = 4
SSD_L = 256

NEG_BIG = -1e30
LOG2E = 1.4426950408889634
Q_SCALE = HD_A ** -0.5 * LOG2E


def _cparams(sem):
    return pltpu.CompilerParams(dimension_semantics=sem, vmem_limit_bytes=VMEM_LIMIT)


def _dot(a, b):
    return jnp.dot(a, b, preferred_element_type=F32)


def _dot_nt(a, b):
    return lax.dot_general(a, b, (((1,), (1,)), ((), ())), preferred_element_type=F32)


def _dot_tn(a, b):
    return lax.dot_general(a, b, (((0,), (0,)), ((), ())), preferred_element_type=F32)


def _split3(x):
    hi = x.astype(BF16)
    r = x - hi.astype(F32)
    mid = r.astype(BF16)
    lo = (r - mid.astype(F32)).astype(BF16)
    return hi, mid, lo


def _dot_exact_rhs(a_bf16, x):
    hi, mid, lo = _split3(x)
    return _dot(a_bf16, hi) + (_dot(a_bf16, mid) + _dot(a_bf16, lo))


def _dot_exact_lhs(x, a_bf16):
    hi, mid, lo = _split3(x)
    return _dot(hi, a_bf16) + (_dot(mid, a_bf16) + _dot(lo, a_bf16))


def _layernorm(y, g, b):
    mu = jnp.mean(y, axis=-1, keepdims=True)
    yc = y - mu
    var = jnp.mean(yc * yc, axis=-1, keepdims=True)
    return yc * lax.rsqrt(var + LN_EPS) * g + b


def _softplus(x):
    return jnp.maximum(x, 0.0) + jnp.log(1.0 + jnp.exp(-jnp.abs(x)))


def _ffn_ln_kernel(x_ref, win_ref, wout_ref, g_ref, b_ref, o_ref):
    x = x_ref[...]
    xb = x.astype(BF16)
    acc = jnp.zeros(x.shape, F32)
    for c in range(D_FF // FFN_CHUNK):
        lo = c * FFN_CHUNK
        gate = _dot(xb, win_ref[:, lo:lo + FFN_CHUNK])
        up = _dot(xb, win_ref[:, D_FF + lo:D_FF + lo + FFN_CHUNK])
        a = (gate * jax.nn.sigmoid(gate) * up).astype(BF16)
        acc = acc + _dot(a, wout_ref[lo:lo + FFN_CHUNK, :])
    y = DEEPNORM_ALPHA * x + FFN_RES * acc
    o_ref[...] = _layernorm(y, g_ref[...], b_ref[...])


def _ffn_ln(x, w_in, w_out, g, b):
    t = x.shape[0]
    tm = min(FFN_TM, t)
    const = lambda i: (0, 0)
    return pl.pallas_call(
        _ffn_ln_kernel,
        grid=(t // tm,),
        in_specs=[
            pl.BlockSpec((tm, D_MODEL), lambda i: (i, 0)),
            pl.BlockSpec((D_MODEL, 2 * D_FF), const, pipeline_mode=pl.Buffered(1)),
            pl.BlockSpec((D_FF, D_MODEL), const, pipeline_mode=pl.Buffered(1)),
            pl.BlockSpec((1, D_MODEL), const),
            pl.BlockSpec((1, D_MODEL), const),
        ],
        out_specs=pl.BlockSpec((tm, D_MODEL), lambda i: (i, 0)),
        out_shape=jax.ShapeDtypeStruct((t, D_MODEL), F32),
        compiler_params=_cparams(("parallel",)),
        name="ffn_ln",
    )(x, w_in, w_out, g.reshape(1, D_MODEL), b.reshape(1, D_MODEL))


def _proj_ssm_kernel(x_ref, w_ref, o_ref):
    xb = x_ref[...].astype(BF16)
    n = w_ref.shape[1]
    step = n // 3
    for lo in range(0, n, step):
        o_ref[:, lo:lo + step] = _dot(xb, w_ref[:, lo:lo + step])


def _proj_ssm(x, w):
    t = x.shape[0]
    n = w.shape[1]
    tm = min(PROJ_TM, t)
    return pl.pallas_call(
        _proj_ssm_kernel,
        grid=(t // tm,),
        in_specs=[
            pl.BlockSpec((tm, D_MODEL), lambda i: (i, 0)),
            pl.BlockSpec((D_MODEL, n), lambda i: (0, 0), pipeline_mode=pl.Buffered(1)),
        ],
        out_specs=pl.BlockSpec((tm, n), lambda i: (i, 0)),
        out_shape=jax.ShapeDtypeStruct((t, n), F32),
        compiler_params=_cparams(("parallel",)),
        name="proj_ssm",
    )(x, w)


def _proj_attn_kernel(x_ref, wqk_ref, watt_ref, c_ref, s1_ref, s2_ref, qk_ref, att_ref):
    xb = x_ref[...].astype(BF16)
    for blk in range(2):
        acc = _dot(xb, wqk_ref[:, blk * A_W:(blk + 1) * A_W])
        scale = Q_SCALE if blk == 0 else 1.0
        c = c_ref[...] * scale
        s1 = s1_ref[...] * scale
        s2 = s2_ref[...] * scale
        for h in range(A_W // LANES):
            a = acc[:, h * LANES:(h + 1) * LANES]
            r = a * c + pltpu.roll(a, 8, axis=1) * s1 + pltpu.roll(a, LANES - 8, axis=1) * s2
            qk_ref[:, blk * A_W + h * LANES:blk * A_W + (h + 1) * LANES] = r.astype(qk_ref.dtype)
    for blk in range(watt_ref.shape[1] // D_MODEL):
        acc = _dot(xb, watt_ref[:, blk * D_MODEL:(blk + 1) * D_MODEL])
        if blk == 1:
            acc = acc * Q_SCALE
        att_ref[:, blk * D_MODEL:(blk + 1) * D_MODEL] = acc.astype(att_ref.dtype)


def _proj_attn(x, w_qk, w_att, tabs, seq):
    t = x.shape[0]
    tm = min(PROJ_TM, seq)
    nrow = seq // tm
    tab_spec = pl.BlockSpec((tm, LANES), lambda i: (i % nrow, 0))
    const = lambda i: (0, 0)
    nqk, natt = w_qk.shape[1], w_att.shape[1]
    return pl.pallas_call(
        _proj_attn_kernel,
        grid=(t // tm,),
        in_specs=[
            pl.BlockSpec((tm, D_MODEL), lambda i: (i, 0)),
            pl.BlockSpec((D_MODEL, nqk), const, pipeline_mode=pl.Buffered(1)),
            pl.BlockSpec((D_MODEL, natt), const, pipeline_mode=pl.Buffered(1)),
            tab_spec, tab_spec, tab_spec,
        ],
        out_specs=[pl.BlockSpec((tm, nqk), lambda i: (i, 0)), pl.BlockSpec((tm, natt), lambda i: (i, 0))],
        out_shape=[jax.ShapeDtypeStruct((t, nqk), BF16), jax.ShapeDtypeStruct((t, natt), BF16)],
        compiler_params=_cparams(("parallel",)),
        name="proj_attn",
    )(x, w_qk, w_att, *tabs)


def _rope_tables(seq):
    half = ROT_A // 2
    inv_freq = ROPE_THETA ** (-jnp.arange(half, dtype=F32) * 2.0 / ROT_A)
    ang = jnp.arange(seq, dtype=F32)[:, None] * inv_freq[None, :]
    cos, sin = jnp.cos(ang), jnp.sin(ang)
    ones = jnp.ones((seq, HD_A - ROT_A), F32)
    zeros = jnp.zeros((seq, HD_A - ROT_A), F32)
    zh = jnp.zeros((seq, half), F32)
    c = jnp.concatenate([cos, cos, ones], axis=1)
    s1 = jnp.concatenate([zh, sin, zeros], axis=1)
    s2 = jnp.concatenate([-sin, zh, zeros], axis=1)
    rep = LANES // HD_A
    return tuple(jnp.tile(v, (1, rep)) for v in (c, s1, s2))


def _diff_attn_kernel(q_ref, k_ref, v_ref, lam_ref, g_ref, o_ref, *, lam_init):
    t = q_ref.shape[0]
    i = pl.program_id(2)
    lane = lax.broadcasted_iota(jnp.int32, (t, LANES), 1)
    q = q_ref[...]
    zero = jnp.zeros_like(q)
    qq = jnp.concatenate([jnp.where(lane < HD_A, q, zero), jnp.where(lane < HD_A, zero, q)], axis=0)

    def step(j, n, carry, masked):
        m, l, acc = carry
        start = pl.multiple_of(j * t, t)
        kb = k_ref[pl.ds(start, n * t), :]
        vb = v_ref[pl.ds(start, n * t), :]
        s = _dot_nt(qq, kb)
        if masked:
            qchunk = lax.broadcasted_iota(jnp.int32, (2 * t, t), 0) % t // CHUNK
            kchunk = lax.broadcasted_iota(jnp.int32, (2 * t, t), 1) // CHUNK
            s = jnp.where(kchunk <= qchunk, s, NEG_BIG)
        m_new = jnp.maximum(m, jnp.max(s, axis=1, keepdims=True))
        p = jnp.exp2(s - m_new)
        corr = jnp.exp2(m - m_new)
        l = corr * l + jnp.sum(p, axis=1, keepdims=True)
        acc = corr * acc + _dot(p.astype(BF16), vb)
        return m_new, l, acc

    carry = (jnp.full((2 * t, 1), NEG_BIG, F32), jnp.zeros((2 * t, 1), F32),
             jnp.zeros((2 * t, LANES), F32))
    done = 0
    n = ATT_A_NT
    while n >= 1:
        cnt = (i - done) // n
        carry = lax.fori_loop(0, cnt, lambda g, c, n=n, done=done: step(done + g * n, n, c, False), carry)
        done = done + cnt * n
        n //= 2
    m, l, acc = step(i, 1, carry, True)
    o = acc / l
    lf = lam_ref[...]
    lam_full = (jnp.exp(jnp.sum(lf[0:1] * lf[1:2], axis=1, keepdims=True))
                - jnp.exp(jnp.sum(lf[2:3] * lf[3:4], axis=1, keepdims=True)) + lam_init)
    d = o[:t] - lam_full * o[t:]
    y = d * lax.rsqrt(jnp.mean(d * d, axis=-1, keepdims=True) + LN_EPS) * g_ref[...]
    o_ref[...] = (y * (1.0 - lam_init)).astype(o_ref.dtype)


def _diff_attn(qk, v_arr, v_col0, lam, subln_g, layer, batch, seq):
    t = min(ATT_A_T, seq)
    lam_init = 0.8 - 0.6 * math.exp(-0.3 * layer)
    kern = functools.partial(_diff_attn_kernel, lam_init=lam_init)
    return pl.pallas_call(
        kern,
        grid=(batch, H_A, seq // t),
        in_specs=[
            pl.BlockSpec((None, t, LANES), lambda b, h, i: (b, i, h)),
            pl.BlockSpec((None, seq, LANES), lambda b, h, i: (b, 0, H_A + h)),
            pl.BlockSpec((None, seq, LANES), lambda b, h, i: (b, 0, v_col0 + h)),
            pl.BlockSpec((4, HD_A), lambda b, h, i: (0, 0)),
            pl.BlockSpec((1, 2 * HD_A), lambda b, h, i: (0, 0)),
        ],
        out_specs=pl.BlockSpec((None, t, LANES), lambda b, h, i: (b, i, h)),
        out_shape=jax.ShapeDtypeStruct((batch, seq, A_W), BF16),
        compiler_params=_cparams(("parallel", "parallel", "parallel")),
        name="diff_attn",
    )(qk, qk, v_arr, lam, subln_g.reshape(1, 2 * HD_A))


def _stick_kernel(q_ref, k_ref, v_ref, o_ref):
    tq = q_ref.shape[0]
    tk = min(ATT_C_TK, tq)
    per = tq // tk
    i = pl.program_id(2)
    lane = lax.broadcasted_iota(jnp.int32, (tq, LANES), 1)
    q = q_ref[...]
    zero = jnp.zeros_like(q)
    qq = jnp.concatenate([jnp.where(lane < HD_C, q, zero), jnp.where(lane < HD_C, zero, q)], axis=0)
    u = jnp.where(lax.broadcasted_iota(jnp.int32, (tk, tk), 0) >= lax.broadcasted_iota(jnp.int32, (tk, tk), 1),
                  -1.0, 0.0).astype(BF16)

    def front(qs, j, tri):
        kb = k_ref[pl.ds(pl.multiple_of(j * tk, tk), tk), :]
        z = _dot_nt(qs, kb)
        neg_abs = lax.bitcast_convert_type(
            lax.bitcast_convert_type(z, jnp.uint32) | jnp.uint32(0x80000000), F32)
        sp = jnp.maximum(z, 0.0) + jnp.log2(1.0 + jnp.exp2(neg_abs))
        if tri is not None:
            sp = jnp.where(tri, sp, 0.0)
        tail = _dot(sp.astype(BF16), u)
        return z + tail, tail[:, 0:1]

    def back(j, w, run, tri):
        a = jnp.exp2(w + run)
        if tri is not None:
            a = jnp.where(tri, a, 0.0)
        return _dot(a.astype(BF16), v_ref[pl.ds(pl.multiple_of(j * tk, tk), tk), :])

    def group(qs, js, tris, carry):
        run, acc = carry
        fronts = [front(qs, j, tri) for j, tri in zip(js, tris)]
        for j, tri, (w, mass) in zip(js, tris, fronts):
            acc = acc + back(j, w, run, tri)
            run = run + mass
        return run, acc

    def full_group(first, n, carry):
        return group(qq, [first - m for m in range(n)], [None] * n, carry)

    tri = (lax.broadcasted_iota(jnp.int32, (2 * tk, tk), 1)
           < lax.broadcasted_iota(jnp.int32, (2 * tk, tk), 0) % tk)
    bands = []
    for r in range(per):
        qs = jnp.concatenate([qq[r * tk:(r + 1) * tk], qq[tq + r * tk:tq + (r + 1) * tk]], axis=0)
        init = (jnp.zeros((2 * tk, 1), F32), jnp.zeros((2 * tk, LANES), F32))
        bands.append(group(qs, [per * i + r - m for m in range(r + 1)], [tri] + [None] * r, init))
    carry = tuple(jnp.concatenate([band[n][:tk] for band in bands] + [band[n][tk:] for band in bands], axis=0)
                  for n in range(2))

    nfull = per * i
    ngrp = nfull // ATT_C_NT
    carry = lax.fori_loop(0, ngrp, lambda n, c: full_group(nfull - 1 - ATT_C_NT * n, ATT_C_NT, c), carry)
    nrest = (nfull - ATT_C_NT * ngrp) // per
    _, acc = lax.fori_loop(0, nrest, lambda n, c: full_group(per * (nrest - n) - 1, per, c), carry)
    o_ref[...] = jnp.where(lane < HD_C, acc[:tq], acc[tq:]).astype(o_ref.dtype)


def _stick_breaking(arr, q_col0, k_col0, v_col0, batch, seq):
    t = min(ATT_C_T, seq)
    nhp = C_W // LANES
    return pl.pallas_call(
        _stick_kernel,
        grid=(batch, nhp, seq // t),
        in_specs=[
            pl.BlockSpec((None, t, LANES), lambda b, h, i: (b, i, q_col0 + h)),
            pl.BlockSpec((None, seq, LANES), lambda b, h, i: (b, 0, k_col0 + h)),
            pl.BlockSpec((None, seq, LANES), lambda b, h, i: (b, 0, v_col0 + h)),
        ],
        out_specs=pl.BlockSpec((None, t, LANES), lambda b, h, i: (b, i, h)),
        out_shape=jax.ShapeDtypeStruct((batch, seq, C_W), BF16),
        compiler_params=_cparams(("parallel", "parallel", "parallel")),
        name="stick_breaking",
    )(arr, arr, arr)


HALO = 8


def _ssd_kernel(z_ref, x_ref, bc_ref, dt_ref, cw_ref, cb_ref, dtb_ref, alog_ref, dskip_ref, ng_ref,
                o_ref, xe_ref, st_ref):
    L = z_ref.shape[0]
    c = pl.program_id(1)

    @pl.when(c == 0)
    def _():
        xe_ref[0:HALO, :] = jnp.zeros((HALO, CONV_DIM), F32)
        st_ref[...] = jnp.zeros(st_ref.shape, F32)

    xe_ref[HALO:HALO + L, 0:D_INNER] = x_ref[...]
    xe_ref[HALO:HALO + L, D_INNER:CONV_DIM] = bc_ref[...]
    conv = cb_ref[...]
    for r in range(CONV_B):
        off = HALO - (CONV_B - 1) + r
        conv = conv + cw_ref[r:r + 1, :] * xe_ref[off:off + L, :]
    xe_ref[0:HALO, :] = xe_ref[L:L + HALO, :]
    xc = conv * jax.nn.sigmoid(conv)
    xs = xc[:, :D_INNER]

    dt = _softplus(dt_ref[...] + dtb_ref[...])
    a = dt * (-jnp.exp(alog_ref[...]))
    row_i = lax.broadcasted_iota(jnp.int32, (L, L), 0)
    col_i = lax.broadcasted_iota(jnp.int32, (L, L), 1)
    causal = col_i <= row_i
    tri = jnp.where(causal, 1.0, 0.0).astype(BF16)
    acum = _dot_exact_rhs(tri, a)
    acum_t = acum.T
    a_last = acum[L - 1:L, :]
    e_in = jnp.exp(acum)
    e_out = jnp.exp(a_last - acum)

    expand = jnp.where(lax.broadcasted_iota(jnp.int32, (LANES, D_INNER), 0)
                       == lax.broadcasted_iota(jnp.int32, (LANES, D_INNER), 1) // P_B,
                       1.0, 0.0).astype(BF16)
    dt_x = _dot_exact_lhs(dt, expand)
    e_in_x = _dot_exact_lhs(e_in, expand)
    e_out_x = _dot_exact_lhs(e_out, expand)

    xdt = xs * dt_x
    xdt_b = xdt.astype(BF16)
    xdt_out_b = (xdt * e_out_x).astype(BF16)
    lane = lax.broadcasted_iota(jnp.int32, (L, LANES), 1)

    hg = H_B // G_B
    gw = D_INNER // G_B
    y_parts = []
    for g in range(G_B):
        bm_f = xc[:, D_INNER + g * N_B:D_INNER + (g + 1) * N_B]
        bm = bm_f.astype(BF16)
        bm_t = bm_f.T.astype(BF16)
        cm = xc[:, D_INNER + G_B * N_B + g * N_B:D_INNER + G_B * N_B + (g + 1) * N_B].astype(BF16)
        cb = _dot_nt(cm, bm)
        st_g = st_ref[:, g * gw:(g + 1) * gw]
        y_off = _dot(cm, st_g.astype(BF16)) * e_in_x[:, g * gw:(g + 1) * gw]
        y_diag = []
        for hp in range(hg // 2):
            h0 = g * hg + 2 * hp
            pair = xdt_b[:, h0 * P_B:(h0 + 2) * P_B]
            ys = []
            for h in (h0, h0 + 1):
                seg = acum[:, h:h + 1] - acum_t[h:h + 1, :]
                m = (cb * jnp.exp(jnp.where(causal, seg, -jnp.inf))).astype(BF16)
                ys.append(_dot(m, pair))
            y_diag.append(jnp.where(lane < P_B, ys[0], ys[1]))
        y_parts.append(jnp.concatenate(y_diag, axis=1) + y_off)
        upd = _dot(bm_t, xdt_out_b[:, g * gw:(g + 1) * gw])
        st_ref[:, g * gw:(g + 1) * gw] = st_g * e_in_x[L - 1:L, g * gw:(g + 1) * gw] + upd

    y = jnp.concatenate(y_parts, axis=1) + dskip_ref[...] * xs
    zz = z_ref[...]
    y = y * (zz * jax.nn.sigmoid(zz))
    outs = []
    for g in range(G_B):
        yg = y[:, g * gw:(g + 1) * gw]
        outs.append(yg * lax.rsqrt(jnp.mean(yg * yg, axis=-1, keepdims=True) + LN_EPS))
    o_ref[...] = (jnp.concatenate(outs, axis=1) * ng_ref[...]).astype(o_ref.dtype)


def _ssd(ssm, conv_w, conv_b, dt_bias, a_log, d_skip, norm_g, batch, seq):
    L = min(SSD_L, seq)
    pad = LANES - H_B
    row = lambda v: v.reshape(1, -1).astype(F32)
    dtb = jnp.pad(row(dt_bias), ((0, 0), (0, pad)))
    alog = jnp.pad(row(a_log), ((0, 0), (0, pad)))
    dskip = jnp.repeat(row(d_skip), P_B, axis=1)
    const = lambda b, c: (0, 0)
    bc_w = CONV_DIM - D_INNER
    return pl.pallas_call(
        _ssd_kernel,
        grid=(batch, seq // L),
        in_specs=[
            pl.BlockSpec((None, L, D_INNER), lambda b, c: (b, c, 0)),
            pl.BlockSpec((None, L, D_INNER), lambda b, c: (b, c, 1)),
            pl.BlockSpec((None, L, bc_w), lambda b, c: (b, c, 2 * D_INNER // bc_w)),
            pl.BlockSpec((None, L, LANES), lambda b, c: (b, c, (D_INNER + CONV_DIM) // LANES)),
            pl.BlockSpec((CONV_B, CONV_DIM), const),
            pl.BlockSpec((1, CONV_DIM), const),
            pl.BlockSpec((1, LANES), const),
            pl.BlockSpec((1, LANES), const),
            pl.BlockSpec((1, D_INNER), const),
            pl.BlockSpec((1, D_INNER), const),
        ],
        out_specs=pl.BlockSpec((None, L, D_INNER), lambda b, c: (b, c, 0)),
        out_shape=jax.ShapeDtypeStruct((batch, seq, D_INNER), BF16),
        scratch_shapes=[pltpu.VMEM((HALO + L, CONV_DIM), F32), pltpu.VMEM((N_B, D_INNER), F32)],
        compiler_params=_cparams(("parallel", "arbitrary")),
        name="ssd",
    )(ssm, ssm, ssm, ssm, conv_w, row(conv_b), dtb, alog, dskip, row(norm_g))


def _merge_ln_kernel(h_ref, oa_ref, ob_ref, oc_ref, wg_ref, gb_ref, wbr_ref, wout_ref, g_ref, b_ref, o_ref):
    h = h_ref[...]
    hb = h.astype(BF16)
    merged = jnp.zeros(h.shape, F32)
    for r, br_ref in enumerate((oa_ref, ob_ref, oc_ref)):
        gpre = _dot(hb, wg_ref[:, r * D_MODEL:(r + 1) * D_MODEL]) + gb_ref[r:r + 1, :]
        merged = merged + jax.nn.sigmoid(gpre) * _dot(br_ref[...], wbr_ref[r])
    mix = _dot(merged.astype(BF16), wout_ref[...])
    o_ref[...] = _layernorm(DEEPNORM_ALPHA * h + mix, g_ref[...], b_ref[...])


def _merge_ln(h, oa, ob, oc, w_gate, gate_bias, w_branch, w_out, g, b):
    t = h.shape[0]
    tm = min(MERGE_TM, t)
    rowspec = pl.BlockSpec((tm, D_MODEL), lambda i: (i, 0))
    const2 = lambda i: (0, 0)
    one = pl.Buffered(1)
    return pl.pallas_call(
        _merge_ln_kernel,
        grid=(t // tm,),
        in_specs=[
            rowspec, rowspec, rowspec, rowspec,
            pl.BlockSpec((D_MODEL, N_BRANCH * D_MODEL), const2, pipeline_mode=one),
            pl.BlockSpec((N_BRANCH, D_MODEL), const2),
            pl.BlockSpec((N_BRANCH, D_MODEL, D_MODEL), lambda i: (0, 0, 0), pipeline_mode=one),
            pl.BlockSpec((D_MODEL, D_MODEL), const2, pipeline_mode=one),
            pl.BlockSpec((1, D_MODEL), const2),
            pl.BlockSpec((1, D_MODEL), const2),
        ],
        out_specs=rowspec,
        out_shape=jax.ShapeDtypeStruct((t, D_MODEL), F32),
        compiler_params=_cparams(("parallel",)),
        name="merge_ln",
    )(h, oa, ob, oc, w_gate, gate_bias, w_branch, w_out, g.reshape(1, D_MODEL), b.reshape(1, D_MODEL))


def _mixer_weights(w_in):
    segs = (A_W, A_W, A_W, D_INNER, CONV_DIM, H_B, C_W, C_W, C_W, N_BRANCH * D_MODEL)
    offs = [0]
    for s in segs:
        offs.append(offs[-1] + s)
    qa, ka, va, zb, xbc, dtb, qc, kc, vc, gate = (w_in[:, offs[n]:offs[n + 1]] for n in range(len(segs)))
    w_qk = jnp.concatenate([qa, ka], axis=1).astype(BF16)
    w_att = jnp.concatenate([va, qc, kc, vc], axis=1).astype(BF16)
    w_ssm = jnp.concatenate([zb, xbc, jnp.pad(dtb, ((0, 0), (0, LANES - H_B)))], axis=1).astype(BF16)
    return w_qk, w_att, w_ssm, gate.astype(BF16)


def kernel(x, ffn1_w_in, ffn1_w_out, ln1_g, ln1_b, w_mix_in, gate_bias, diff_lambda, diff_subln_g,
           ssm_conv_w, ssm_conv_b, ssm_dt_bias, ssm_A_log, ssm_D, ssm_norm_g, w_branch, w_mix_out,
           ln2_g, ln2_b, ffn2_w_in, ffn2_w_out, ln3_g, ln3_b):
    batch, seq, _ = x.shape
    t = batch * seq
    tabs = _rope_tables(seq)
    xf = x.reshape(t, D_MODEL)
    for l in range(DEPTH):
        xf = _ffn_ln(xf, ffn1_w_in[l].astype(BF16), ffn1_w_out[l].astype(BF16), ln1_g[l], ln1_b[l])

        w_qk, w_att, w_ssm, w_gate = _mixer_weights(w_mix_in[l])
        qk, att = _proj_attn(xf, w_qk, w_att, tabs, seq)
        qk = qk.reshape(batch, seq, 2 * A_W)
        att = att.reshape(batch, seq, A_W + 3 * C_W)
        ssm = _proj_ssm(xf, w_ssm).reshape(batch, seq, w_ssm.shape[1])

        o_a = _diff_attn(qk, att, 0, diff_lambda[l], diff_subln_g[l], l, batch, seq)
        nb = A_W // LANES
        o_c = _stick_breaking(att, nb, 2 * nb, 3 * nb, batch, seq)
        o_b = _ssd(ssm, ssm_conv_w[l], ssm_conv_b[l], ssm_dt_bias[l], ssm_A_log[l], ssm_D[l],
                   ssm_norm_g[l], batch, seq)

        xf = _merge_ln(xf, o_a.reshape(t, A_W), o_b.reshape(t, D_INNER), o_c.reshape(t, C_W),
                       w_gate, gate_bias[l], w_branch[l].astype(BF16), w_mix_out[l].astype(BF16),
                       ln2_g[l], ln2_b[l])

        xf = _ffn_ln(xf, ffn2_w_in[l].astype(BF16), ffn2_w_out[l].astype(BF16), ln3_g[l], ln3_b[l])
    return xf.reshape(batch, seq, D_MODEL)
```

```python
import functools
import math

import jax
import jax.numpy as jnp
from jax import lax
from jax.experimental import pallas as pl
from jax.experimental.pallas import tpu as pltpu

F32 = jnp.float32
BF16 = jnp.bfloat16

D_MODEL = 1024
DEPTH = 2
LN_EPS = 1e-5
DEEPNORM_ALPHA = (2 * DEPTH) ** 0.25
D_FF = 2816
FFN_RES = 0.5

CHUNK = 64
H_A = 8
HD_A = 64
ROT_A = HD_A // 4
ROPE_THETA = 500000.0
A_W = H_A * 2 * HD_A

H_B = 16
P_B = 64
D_INNER = H_B * P_B
G_B = 2
N_B = 128
CONV_B = 4
CONV_DIM = D_INNER + 2 * G_B * N_B

H_C = 16
HD_C = 64
C_W = H_C * HD_C
N_BRANCH = 3

LANES = 128
VMEM_LIMIT = 56 * 1024 * 1024

FFN_TM = 512
FFN_CHUNK = 256
PROJ_TM = 512
MERGE_TM = 512
ATT_A_T = 512
ATT_A_NT = 4
ATT_C_T = 512
ATT_C_TK = 256
ATT_C_NT = 4
SSD_L = 256

NEG_BIG = -1e30
LOG2E = 1.4426950408889634
Q_SCALE = HD_A ** -0.5 * LOG2E


def _cparams(sem):
    return pltpu.CompilerParams(dimension_semantics=sem, vmem_limit_bytes=VMEM_LIMIT)


def _dot(a, b):
    return jnp.dot(a, b, preferred_element_type=F32)


def _dot_nt(a, b):
    return lax.dot_general(a, b, (((1,), (1,)), ((), ())), preferred_element_type=F32)


def _dot_tn(a, b):
    return lax.dot_general(a, b, (((0,), (0,)), ((), ())), preferred_element_type=F32)


def _split3(x):
    hi = x.astype(BF16)
    r = x - hi.astype(F32)
    mid = r.astype(BF16)
    lo = (r - mid.astype(F32)).astype(BF16)
    return hi, mid, lo


def _dot_exact_rhs(a_bf16, x):
    hi, mid, lo = _split3(x)
    return _dot(a_bf16, hi) + (_dot(a_bf16, mid) + _dot(a_bf16, lo))


def _dot_exact_lhs(x, a_bf16):
    hi, mid, lo = _split3(x)
    return _dot(hi, a_bf16) + (_dot(mid, a_bf16) + _dot(lo, a_bf16))


def _layernorm(y, g, b):
    mu = jnp.mean(y, axis=-1, keepdims=True)
    yc = y - mu
    var = jnp.mean(yc * yc, axis=-1, keepdims=True)
    return yc * lax.rsqrt(var + LN_EPS) * g + b


def _softplus(x):
    return jnp.maximum(x, 0.0) + jnp.log(1.0 + jnp.exp(-jnp.abs(x)))


def _ffn_ln_kernel(x_ref, win_ref, wout_ref, g_ref, b_ref, o_ref):
    x = x_ref[...]
    xb = x.astype(BF16)
    acc = jnp.zeros(x.shape, F32)
    for c in range(D_FF // FFN_CHUNK):
        lo = c * FFN_CHUNK
        gate = _dot(xb, win_ref[:, lo:lo + FFN_CHUNK])
        up = _dot(xb, win_ref[:, D_FF + lo:D_FF + lo + FFN_CHUNK])
        a = (gate * jax.nn.sigmoid(gate) * up).astype(BF16)
        acc = acc + _dot(a, wout_ref[lo:lo + FFN_CHUNK, :])
    y = DEEPNORM_ALPHA * x + FFN_RES * acc
    o_ref[...] = _layernorm(y, g_ref[...], b_ref[...])


def _ffn_ln(x, w_in, w_out, g, b):
    t = x.shape[0]
    tm = min(FFN_TM, t)
    const = lambda i: (0, 0)
    return pl.pallas_call(
        _ffn_ln_kernel,
        grid=(t // tm,),
        in_specs=[
            pl.BlockSpec((tm, D_MODEL), lambda i: (i, 0)),
            pl.BlockSpec((D_MODEL, 2 * D_FF), const, pipeline_mode=pl.Buffered(1)),
            pl.BlockSpec((D_FF, D_MODEL), const, pipeline_mode=pl.Buffered(1)),
            pl.BlockSpec((1, D_MODEL), const),
            pl.BlockSpec((1, D_MODEL), const),
        ],
        out_specs=pl.BlockSpec((tm, D_MODEL), lambda i: (i, 0)),
        out_shape=jax.ShapeDtypeStruct((t, D_MODEL), F32),
        compiler_params=_cparams(("parallel",)),
        name="ffn_ln",
    )(x, w_in, w_out, g.reshape(1, D_MODEL), b.reshape(1, D_MODEL))


def _proj_ssm_kernel(x_ref, w_ref, o_ref):
    xb = x_ref[...].astype(BF16)
    n = w_ref.shape[1]
    step = n // 3
    for lo in range(0, n, step):
        o_ref[:, lo:lo + step] = _dot(xb, w_ref[:, lo:lo + step])


def _proj_ssm(x, w):
    t = x.shape[0]
    n = w.shape[1]
    tm = min(PROJ_TM, t)
    return pl.pallas_call(
        _proj_ssm_kernel,
        grid=(t // tm,),
        in_specs=[
            pl.BlockSpec((tm, D_MODEL), lambda i: (i, 0)),
            pl.BlockSpec((D_MODEL, n), lambda i: (0, 0), pipeline_mode=pl.Buffered(1)),
        ],
        out_specs=pl.BlockSpec((tm, n), lambda i: (i, 0)),
        out_shape=jax.ShapeDtypeStruct((t, n), F32),
        compiler_params=_cparams(("parallel",)),
        name="proj_ssm",
    )(x, w)


def _proj_attn_kernel(x_ref, wqk_ref, watt_ref, c_ref, s1_ref, s2_ref, qk_ref, att_ref):
    xb = x_ref[...].astype(BF16)
    for blk in range(2):
        acc = _dot(xb, wqk_ref[:, blk * A_W:(blk + 1) * A_W])
        scale = Q_SCALE if blk == 0 else 1.0
        c = c_ref[...] * scale
        s1 = s1_ref[...] * scale
        s2 = s2_ref[...] * scale
        for h in range(A_W // LANES):
            a = acc[:, h * LANES:(h + 1) * LANES]
            r = a * c + pltpu.roll(a, 8, axis=1) * s1 + pltpu.roll(a, LANES - 8, axis=1) * s2
            qk_ref[:, blk * A_W + h * LANES:blk * A_W + (h + 1) * LANES] = r.astype(qk_ref.dtype)
    for blk in range(watt_ref.shape[1] // D_MODEL):
        acc = _dot(xb, watt_ref[:, blk * D_MODEL:(blk + 1) * D_MODEL])
        if blk == 1:
            acc = acc * Q_SCALE
        att_ref[:, blk * D_MODEL:(blk + 1) * D_MODEL] = acc.astype(att_ref.dtype)


def _proj_attn(x, w_qk, w_att, tabs, seq):
    t = x.shape[0]
    tm = min(PROJ_TM, seq)
    nrow = seq // tm
    tab_spec = pl.BlockSpec((tm, LANES), lambda i: (i % nrow, 0))
    const = lambda i: (0, 0)
    nqk, natt = w_qk.shape[1], w_att.shape[1]
    return pl.pallas_call(
        _proj_attn_kernel,
        grid=(t // tm,),
        in_specs=[
            pl.BlockSpec((tm, D_MODEL), lambda i: (i, 0)),
            pl.BlockSpec((D_MODEL, nqk), const, pipeline_mode=pl.Buffered(1)),
            pl.BlockSpec((D_MODEL, natt), const, pipeline_mode=pl.Buffered(1)),
            tab_spec, tab_spec, tab_spec,
        ],
        out_specs=[pl.BlockSpec((tm, nqk), lambda i: (i, 0)), pl.BlockSpec((tm, natt), lambda i: (i, 0))],
        out_shape=[jax.ShapeDtypeStruct((t, nqk), BF16), jax.ShapeDtypeStruct((t, natt), BF16)],
        compiler_params=_cparams(("parallel",)),
        name="proj_attn",
    )(x, w_qk, w_att, *tabs)


def _rope_tables(seq):
    half = ROT_A // 2
    inv_freq = ROPE_THETA ** (-jnp.arange(half, dtype=F32) * 2.0 / ROT_A)
    ang = jnp.arange(seq, dtype=F32)[:, None] * inv_freq[None, :]
    cos, sin = jnp.cos(ang), jnp.sin(ang)
    ones = jnp.ones((seq, HD_A - ROT_A), F32)
    zeros = jnp.zeros((seq, HD_A - ROT_A), F32)
    zh = jnp.zeros((seq, half), F32)
    c = jnp.concatenate([cos, cos, ones], axis=1)
    s1 = jnp.concatenate([zh, sin, zeros], axis=1)
    s2 = jnp.concatenate([-sin, zh, zeros], axis=1)
    rep = LANES // HD_A
    return tuple(jnp.tile(v, (1, rep)) for v in (c, s1, s2))


def _diff_attn_kernel(q_ref, k_ref, v_ref, lam_ref, g_ref, o_ref, qq_ref, m_ref, l_ref, acc_ref, *, lam_init):
    t = q_ref.shape[0]
    i = pl.program_id(2)
    lane = lax.broadcasted_iota(jnp.int32, (t, LANES), 1)
    q = q_ref[...]
    zero = jnp.zeros_like(q)
    qq_ref[...] = jnp.concatenate([jnp.where(lane < HD_A, q, zero), jnp.where(lane < HD_A, zero, q)], axis=0)
    m_ref[...] = jnp.full(m_ref.shape, NEG_BIG, F32)
    l_ref[...] = jnp.zeros(l_ref.shape, F32)
    acc_ref[...] = jnp.zeros(acc_ref.shape, F32)

    def step(j, n, masked):
        m = m_ref[...]
        start = pl.multiple_of(j * t, t)
        kb = k_ref[pl.ds(start, n * t), :]
        vb = v_ref[pl.ds(start, n * t), :]
        s = _dot_nt(qq_ref[...], kb)
        if masked:
            qchunk = lax.broadcasted_iota(jnp.int32, (2 * t, t), 0) % t // CHUNK
            kchunk = lax.broadcasted_iota(jnp.int32, (2 * t, t), 1) // CHUNK
            s = jnp.where(kchunk <= qchunk, s, NEG_BIG)
        m_new = jnp.maximum(m, jnp.max(s, axis=1, keepdims=True))
        p = jnp.concatenate([jnp.exp2(s[:, c:c + LANES] - m_new) for c in range(0, n * t, LANES)], axis=1)
        corr = jnp.exp2(m - m_new)
        l_ref[...] = corr * l_ref[...] + jnp.sum(p, axis=1, keepdims=True)
        acc_ref[...] = corr * acc_ref[...] + _dot(p.astype(BF16), vb)
        m_ref[...] = m_new

    done = 0
    n = ATT_A_NT
    while n >= 1:
        cnt = (i - done) // n

        def body(g, c, n=n, done=done):
            step(done + g * n, n, False)
            return c

        lax.fori_loop(0, cnt, body, 0)
        done = done + cnt * n
        n //= 2
    step(i, 1, True)
    o = acc_ref[...] / l_ref[...]
    lf = lam_ref[...]
    lam_full = (jnp.exp(jnp.sum(lf[0:1] * lf[1:2], axis=1, keepdims=True))
                - jnp.exp(jnp.sum(lf[2:3] * lf[3:4], axis=1, keepdims=True)) + lam_init)
    d = o[:t] - lam_full * o[t:]
    y = d * lax.rsqrt(jnp.mean(d * d, axis=-1, keepdims=True) + LN_EPS) * g_ref[...]
    o_ref[...] = (y * (1.0 - lam_init)).astype(o_ref.dtype)


def _diff_attn(qk, v_arr, v_col0, lam, subln_g, layer, batch, seq):
    t = min(ATT_A_T, seq)
    lam_init = 0.8 - 0.6 * math.exp(-0.3 * layer)
    kern = functools.partial(_diff_attn_kernel, lam_init=lam_init)
    return pl.pallas_call(
        kern,
        grid=(batch, H_A, seq // t),
        in_specs=[
            pl.BlockSpec((None, t, LANES), lambda b, h, i: (b, i, h)),
            pl.BlockSpec((None, seq, LANES), lambda b, h, i: (b, 0, H_A + h)),
            pl.BlockSpec((None, seq, LANES), lambda b, h, i: (b, 0, v_col0 + h)),
            pl.BlockSpec((4, HD_A), lambda b, h, i: (0, 0)),
            pl.BlockSpec((1, 2 * HD_A), lambda b, h, i: (0, 0)),
        ],
        out_specs=pl.BlockSpec((None, t, LANES), lambda b, h, i: (b, i, h)),
        out_shape=jax.ShapeDtypeStruct((batch, seq, A_W), BF16),
        scratch_shapes=[pltpu.VMEM((2 * t, LANES), BF16)] + [pltpu.VMEM((2 * t, LANES), F32)] * 3,
        compiler_params=_cparams(("parallel", "parallel", "parallel")),
        name="diff_attn",
    )(qk, qk, v_arr, lam, subln_g.reshape(1, 2 * HD_A))


def _stick_kernel(q_ref, k_ref, v_ref, o_ref, qq_ref, run_ref, acc_ref):
    tq = q_ref.shape[0]
    tk = min(ATT_C_TK, tq)
    per = tq // tk
    i = pl.program_id(2)
    lane = lax.broadcasted_iota(jnp.int32, (tq, LANES), 1)
    q = q_ref[...]
    zero = jnp.zeros_like(q)
    qq_ref[...] = jnp.concatenate([jnp.where(lane < HD_C, q, zero), jnp.where(lane < HD_C, zero, q)], axis=0)
    u = jnp.where(lax.broadcasted_iota(jnp.int32, (tk, tk), 0) >= lax.broadcasted_iota(jnp.int32, (tk, tk), 1),
                  -1.0, 0.0).astype(BF16)

    def front(qs, j, tri):
        kb = k_ref[pl.ds(pl.multiple_of(j * tk, tk), tk), :]
        z = _dot_nt(qs, kb)
        neg_abs = lax.bitcast_convert_type(
            lax.bitcast_convert_type(z, jnp.uint32) | jnp.uint32(0x80000000), F32)
        sp = jnp.maximum(z, 0.0) + jnp.log2(1.0 + jnp.exp2(neg_abs))
        if tri is not None:
            sp = jnp.where(tri, sp, 0.0)
        tail = _dot(sp.astype(BF16), u)
        return z + tail, tail[:, 0:1]

    def back(j, w, run, tri):
        a = jnp.concatenate([jnp.exp2(w[:, c:c + LANES] + run) for c in range(0, tk, LANES)], axis=1)
        if tri is not None:
            a = jnp.where(tri, a, 0.0)
        return _dot(a.astype(BF16), v_ref[pl.ds(pl.multiple_of(j * tk, tk), tk), :])

    def group(qs, js, tris, run, acc):
        fronts = [front(qs, j, tri) for j, tri in zip(js, tris)]
        for j, tri, (w, mass) in zip(js, tris, fronts):
            acc = acc + back(j, w, run, tri)
            run = run + mass
        return run, acc

    def full_group(first, n):
        run, acc = group(qq_ref[...], [first - m for m in range(n)], [None] * n, run_ref[...], acc_ref[...])
        run_ref[...] = run
        acc_ref[...] = acc

    tri = (lax.broadcasted_iota(jnp.int32, (2 * tk, tk), 1)
           < lax.broadcasted_iota(jnp.int32, (2 * tk, tk), 0) % tk)
    zeros = jnp.zeros((2 * tk, LANES), F32)
    for r in range(per):
        even, odd = pl.ds(r * tk, tk), pl.ds(tq + r * tk, tk)
        qs = jnp.concatenate([qq_ref[even, :], qq_ref[odd, :]], axis=0)
        run, acc = group(qs, [per * i + r - m for m in range(r + 1)], [tri] + [None] * r, zeros, zeros)
        run_ref[even, :], run_ref[odd, :] = run[:tk], run[tk:]
        acc_ref[even, :], acc_ref[odd, :] = acc[:tk], acc[tk:]

    nfull = per * i
    ngrp = nfull // ATT_C_NT
    nrest = (nfull - ATT_C_NT * ngrp) // per

    def grp_body(n, c):
        full_group(nfull - 1 - ATT_C_NT * n, ATT_C_NT)
        return c

    def rest_body(n, c):
        full_group(per * (nrest - n) - 1, per)
        return c

    lax.fori_loop(0, ngrp, grp_body, 0)
    lax.fori_loop(0, nrest, rest_body, 0)
    o_ref[...] = jnp.where(lane < HD_C, acc_ref[0:tq, :], acc_ref[tq:2 * tq, :]).astype(o_ref.dtype)


def _stick_breaking(arr, q_col0, k_col0, v_col0, batch, seq):
    t = min(ATT_C_T, seq)
    nhp = C_W // LANES
    return pl.pallas_call(
        _stick_kernel,
        grid=(batch, nhp, seq // t),
        in_specs=[
            pl.BlockSpec((None, t, LANES), lambda b, h, i: (b, i, q_col0 + h)),
            pl.BlockSpec((None, seq, LANES), lambda b, h, i: (b, 0, k_col0 + h)),
            pl.BlockSpec((None, seq, LANES), lambda b, h, i: (b, 0, v_col0 + h)),
        ],
        out_specs=pl.BlockSpec((None, t, LANES), lambda b, h, i: (b, i, h)),
        out_shape=jax.ShapeDtypeStruct((batch, seq, C_W), BF16),
        scratch_shapes=[pltpu.VMEM((2 * t, LANES), BF16)] + [pltpu.VMEM((2 * t, LANES), F32)] * 2,
        compiler_params=_cparams(("parallel", "parallel", "parallel")),
        name="stick_breaking",
    )(arr, arr, arr)


HALO = 8


def _ssd_kernel(z_ref, x_ref, bc_ref, dt_ref, cw_ref, cb_ref, dtb_ref, alog_ref, dskip_ref, ng_ref,
                o_ref, xe_ref, st_ref):
    L = z_ref.shape[0]
    c = pl.program_id(1)

    @pl.when(c == 0)
    def _():
        xe_ref[0:HALO, :] = jnp.zeros((HALO, CONV_DIM), F32)
        st_ref[...] = jnp.zeros(st_ref.shape, F32)

    xe_ref[HALO:HALO + L, 0:D_INNER] = x_ref[...]
    xe_ref[HALO:HALO + L, D_INNER:CONV_DIM] = bc_ref[...]
    conv = cb_ref[...]
    for r in range(CONV_B):
        off = HALO - (CONV_B - 1) + r
        conv = conv + cw_ref[r:r + 1, :] * xe_ref[off:off + L, :]
    xe_ref[0:HALO, :] = xe_ref[L:L + HALO, :]
    xc = conv * jax.nn.sigmoid(conv)
    xs = xc[:, :D_INNER]

    dt = _softplus(dt_ref[...] + dtb_ref[...])
    a = dt * (-jnp.exp(alog_ref[...]))
    row_i = lax.broadcasted_iota(jnp.int32, (L, L), 0)
    col_i = lax.broadcasted_iota(jnp.int32, (L, L), 1)
    causal = col_i <= row_i
    tri = jnp.where(causal, 1.0, 0.0).astype(BF16)
    acum = _dot_exact_rhs(tri, a)
    acum_t = acum.T
    a_last = acum[L - 1:L, :]
    e_in = jnp.exp(acum)
    e_out = jnp.exp(a_last - acum)

    expand = jnp.where(lax.broadcasted_iota(jnp.int32, (LANES, D_INNER), 0)
                       == lax.broadcasted_iota(jnp.int32, (LANES, D_INNER), 1) // P_B,
                       1.0, 0.0).astype(BF16)
    dt_x = _dot_exact_lhs(dt, expand)
    e_in_x = _dot_exact_lhs(e_in, expand)
    e_out_x = _dot_exact_lhs(e_out, expand)

    xdt = xs * dt_x
    xdt_b = xdt.astype(BF16)
    xdt_out_b = (xdt * e_out_x).astype(BF16)
    lane = lax.broadcasted_iota(jnp.int32, (L, LANES), 1)

    hg = H_B // G_B
    gw = D_INNER // G_B
    y_parts = []
    for g in range(G_B):
        bm_f = xc[:, D_INNER + g * N_B:D_INNER + (g + 1) * N_B]
        bm = bm_f.astype(BF16)
        bm_t = bm_f.T.astype(BF16)
        cm = xc[:, D_INNER + G_B * N_B + g * N_B:D_INNER + G_B * N_B + (g + 1) * N_B].astype(BF16)
        cb = _dot_nt(cm, bm)
        st_g = st_ref[:, g * gw:(g + 1) * gw]
        y_off = _dot(cm, st_g.astype(BF16)) * e_in_x[:, g * gw:(g + 1) * gw]
        y_diag = []
        for hp in range(hg // 2):
            h0 = g * hg + 2 * hp
            pair = xdt_b[:, h0 * P_B:(h0 + 2) * P_B]
            ys = []
            for h in (h0, h0 + 1):
                seg = acum[:, h:h + 1] - acum_t[h:h + 1, :]
                m = (cb * jnp.exp(jnp.where(causal, seg, -jnp.inf))).astype(BF16)
                ys.append(_dot(m, pair))
            y_diag.append(jnp.where(lane < P_B, ys[0], ys[1]))
        y_parts.append(jnp.concatenate(y_diag, axis=1) + y_off)
        upd = _dot(bm_t, xdt_out_b[:, g * gw:(g + 1) * gw])
        st_ref[:, g * gw:(g + 1) * gw] = st_g * e_in_x[L - 1:L, g * gw:(g + 1) * gw] + upd

    y = jnp.concatenate(y_parts, axis=1) + dskip_ref[...] * xs
    zz = z_ref[...]
    y = y * (zz * jax.nn.sigmoid(zz))
    outs = []
    for g in range(G_B):
        yg = y[:, g * gw:(g + 1) * gw]
        outs.append(yg * lax.rsqrt(jnp.mean(yg * yg, axis=-1, keepdims=True) + LN_EPS))
    o_ref[...] = (jnp.concatenate(outs, axis=1) * ng_ref[...]).astype(o_ref.dtype)


def _ssd(ssm, conv_w, conv_b, dt_bias, a_log, d_skip, norm_g, batch, seq):
    L = min(SSD_L, seq)
    pad = LANES - H_B
    row = lambda v: v.reshape(1, -1).astype(F32)
    dtb = jnp.pad(row(dt_bias), ((0, 0), (0, pad)))
    alog = jnp.pad(row(a_log), ((0, 0), (0, pad)))
    dskip = jnp.repeat(row(d_skip), P_B, axis=1)
    const = lambda b, c: (0, 0)
    bc_w = CONV_DIM - D_INNER
    return pl.pallas_call(
        _ssd_kernel,
        grid=(batch, seq // L),
        in_specs=[
            pl.BlockSpec((None, L, D_INNER), lambda b, c: (b, c, 0)),
            pl.BlockSpec((None, L, D_INNER), lambda b, c: (b, c, 1)),
            pl.BlockSpec((None, L, bc_w), lambda b, c: (b, c, 2 * D_INNER // bc_w)),
            pl.BlockSpec((None, L, LANES), lambda b, c: (b, c, (D_INNER + CONV_DIM) // LANES)),
            pl.BlockSpec((CONV_B, CONV_DIM), const),
            pl.BlockSpec((1, CONV_DIM), const),
            pl.BlockSpec((1, LANES), const),
            pl.BlockSpec((1, LANES), const),
            pl.BlockSpec((1, D_INNER), const),
            pl.BlockSpec((1, D_INNER), const),
        ],
        out_specs=pl.BlockSpec((None, L, D_INNER), lambda b, c: (b, c, 0)),
        out_shape=jax.ShapeDtypeStruct((batch, seq, D_INNER), BF16),
        scratch_shapes=[pltpu.VMEM((HALO + L, CONV_DIM), F32), pltpu.VMEM((N_B, D_INNER), F32)],
        compiler_params=_cparams(("parallel", "arbitrary")),
        name="ssd",
    )(ssm, ssm, ssm, ssm, conv_w, row(conv_b), dtb, alog, dskip, row(norm_g))


def _merge_ln_kernel(h_ref, oa_ref, ob_ref, oc_ref, wg_ref, gb_ref, wbr_ref, wout_ref, g_ref, b_ref, o_ref):
    h = h_ref[...]
    hb = h.astype(BF16)
    merged = jnp.zeros(h.shape, F32)
    for r, br_ref in enumerate((oa_ref, ob_ref, oc_ref)):
        gpre = _dot(hb, wg_ref[:, r * D_MODEL:(r + 1) * D_MODEL]) + gb_ref[r:r + 1, :]
        merged = merged + jax.nn.sigmoid(gpre) * _dot(br_ref[...], wbr_ref[r])
    mix = _dot(merged.astype(BF16), wout_ref[...])
    o_ref[...] = _layernorm(DEEPNORM_ALPHA * h + mix, g_ref[...], b_ref[...])


def _merge_ln(h, oa, ob, oc, w_gate, gate_bias, w_branch, w_out, g, b):
    t = h.shape[0]
    tm = min(MERGE_TM, t)
    rowspec = pl.BlockSpec((tm, D_MODEL), lambda i: (i, 0))
    const2 = lambda i: (0, 0)
    one = pl.Buffered(1)
    return pl.pallas_call(
        _merge_ln_kernel,
        grid=(t // tm,),
        in_specs=[
            rowspec, rowspec, rowspec, rowspec,
            pl.BlockSpec((D_MODEL, N_BRANCH * D_MODEL), const2, pipeline_mode=one),
            pl.BlockSpec((N_BRANCH, D_MODEL), const2),
            pl.BlockSpec((N_BRANCH, D_MODEL, D_MODEL), lambda i: (0, 0, 0), pipeline_mode=one),
            pl.BlockSpec((D_MODEL, D_MODEL), const2, pipeline_mode=one),
            pl.BlockSpec((1, D_MODEL), const2),
            pl.BlockSpec((1, D_MODEL), const2),
        ],
        out_specs=rowspec,
        out_shape=jax.ShapeDtypeStruct((t, D_MODEL), F32),
        compiler_params=_cparams(("parallel",)),
        name="merge_ln",
    )(h, oa, ob, oc, w_gate, gate_bias, w_branch, w_out, g.reshape(1, D_MODEL), b.reshape(1, D_MODEL))


def _mixer_weights(w_in):
    segs = (A_W, A_W, A_W, D_INNER, CONV_DIM, H_B, C_W, C_W, C_W, N_BRANCH * D_MODEL)
    offs = [0]
    for s in segs:
        offs.append(offs[-1] + s)
    qa, ka, va, zb, xbc, dtb, qc, kc, vc, gate = (w_in[:, offs[n]:offs[n + 1]] for n in range(len(segs)))
    w_qk = jnp.concatenate([qa, ka], axis=1).astype(BF16)
    w_att = jnp.concatenate([va, qc, kc, vc], axis=1).astype(BF16)
    w_ssm = jnp.concatenate([zb, xbc, jnp.pad(dtb, ((0, 0), (0, LANES - H_B)))], axis=1).astype(BF16)
    return w_qk, w_att, w_ssm, gate.astype(BF16)


def kernel(x, ffn1_w_in, ffn1_w_out, ln1_g, ln1_b, w_mix_in, gate_bias, diff_lambda, diff_subln_g,
           ssm_conv_w, ssm_conv_b, ssm_dt_bias, ssm_A_log, ssm_D, ssm_norm_g, w_branch, w_mix_out,
           ln2_g, ln2_b, ffn2_w_in, ffn2_w_out, ln3_g, ln3_b):
    batch, seq, _ = x.shape
    t = batch * seq
    tabs = _rope_tables(seq)
    xf = x.reshape(t, D_MODEL)
    for l in range(DEPTH):
        xf = _ffn_ln(xf, ffn1_w_in[l].astype(BF16), ffn1_w_out[l].astype(BF16), ln1_g[l], ln1_b[l])

        w_qk, w_att, w_ssm, w_gate = _mixer_weights(w_mix_in[l])
        qk, att = _proj_attn(xf, w_qk, w_att, tabs, seq)
        qk = qk.reshape(batch, seq, 2 * A_W)
        att = att.reshape(batch, seq, A_W + 3 * C_W)
        ssm = _proj_ssm(xf, w_ssm).reshape(batch, seq, w_ssm.shape[1])

        o_a = _diff_attn(qk, att, 0, diff_lambda[l], diff_subln_g[l], l, batch, seq)
        nb = A_W // LANES
        o_c = _stick_breaking(att, nb, 2 * nb, 3 * nb, batch, seq)
        o_b = _ssd(ssm, ssm_conv_w[l], ssm_conv_b[l], ssm_dt_bias[l], ssm_A_log[l], ssm_D[l],
                   ssm_norm_g[l], batch, seq)

        xf = _merge_ln(xf, o_a.reshape(t, A_W), o_b.reshape(t, D_INNER), o_c.reshape(t, C_W),
                       w_gate, gate_bias[l], w_branch[l].astype(BF16), w_mix_out[l].astype(BF16),
                       ln2_g[l], ln2_b[l])

        xf = _ffn_ln(xf, ffn2_w_in[l].astype(BF16), ffn2_w_out[l].astype(BF16), ln3_g[l], ln3_b[l])
    return xf.reshape(batch, seq, D_MODEL)
```

```python
import functools
import math

import jax
import jax.numpy as jnp
from jax import lax
from jax.experimental import pallas as pl
from jax.experimental.pallas import tpu as pltpu

F32 = jnp.float32
BF16 = jnp.bfloat16

D_MODEL = 1024
DEPTH = 2
LN_EPS = 1e-5
DEEPNORM_ALPHA = (2 * DEPTH) ** 0.25
D_FF = 2816
FFN_RES = 0.5

CHUNK = 64
H_A = 8
HD_A = 64
ROT_A = HD_A // 4
ROPE_THETA = 500000.0
A_W = H_A * 2 * HD_A

H_B = 16
P_B = 64
D_INNER = H_B * P_B
G_B = 2
N_B = 128
CONV_B = 4
CONV_DIM = D_INNER + 2 * G_B * N_B

H_C = 16
HD_C = 64
C_W = H_C * HD_C
N_BRANCH = 3

LANES = 128
VMEM_LIMIT = 56 * 1024 * 1024

FFN_TM = 512
FFN_CHUNK = 256
PROJ_TM = 512
PROJ_SSM_CHUNKS = 3
MERGE_TM = 512
ATT_A_T = 512
ATT_A_NT = 4
ATT_C_T = 512
ATT_C_TK = 256
ATT_C_NT = 4
SSD_L = 256

assert D_FF % FFN_CHUNK == 0
assert ATT_C_T % ATT_C_TK == 0 and ATT_C_NT % (ATT_C_T // ATT_C_TK) == 0
assert HD_A == HD_C

NEG_BIG = -1e30
LOG2E = 1.4426950408889634
Q_SCALE = HD_A ** -0.5 * LOG2E


def _cparams(sem):
    return pltpu.CompilerParams(dimension_semantics=sem, vmem_limit_bytes=VMEM_LIMIT)


def _dot(a, b):
    return jnp.dot(a, b, preferred_element_type=F32)


def _dot_nt(a, b):
    return lax.dot_general(a, b, (((1,), (1,)), ((), ())), preferred_element_type=F32)


def _split3(x):
    hi = x.astype(BF16)
    r = x - hi.astype(F32)
    mid = r.astype(BF16)
    lo = (r - mid.astype(F32)).astype(BF16)
    return hi, mid, lo


def _dot_exact_rhs(a_bf16, x):
    hi, mid, lo = _split3(x)
    return _dot(a_bf16, hi) + (_dot(a_bf16, mid) + _dot(a_bf16, lo))


def _dot_exact_lhs(x, a_bf16):
    hi, mid, lo = _split3(x)
    return _dot(hi, a_bf16) + (_dot(mid, a_bf16) + _dot(lo, a_bf16))


def _layernorm(y, g, b):
    mu = jnp.mean(y, axis=-1, keepdims=True)
    yc = y - mu
    var = jnp.mean(yc * yc, axis=-1, keepdims=True)
    return yc * lax.rsqrt(var + LN_EPS) * g + b


def _softplus(x):
    return jnp.maximum(x, 0.0) + jnp.log(1.0 + jnp.exp(-jnp.abs(x)))


def _ffn_ln_kernel(x_ref, win_ref, wout_ref, g_ref, b_ref, o_ref):
    x = x_ref[...]
    xb = x.astype(BF16)
    acc = jnp.zeros(x.shape, F32)
    for c in range(D_FF // FFN_CHUNK):
        lo = c * FFN_CHUNK
        gate = _dot(xb, win_ref[:, lo:lo + FFN_CHUNK])
        up = _dot(xb, win_ref[:, D_FF + lo:D_FF + lo + FFN_CHUNK])
        a = (gate * jax.nn.sigmoid(gate) * up).astype(BF16)
        acc = acc + _dot(a, wout_ref[lo:lo + FFN_CHUNK, :])
    y = DEEPNORM_ALPHA * x + FFN_RES * acc
    o_ref[...] = _layernorm(y, g_ref[...], b_ref[...])


def _ffn_ln(x, w_in, w_out, g, b):
    t = x.shape[0]
    tm = min(FFN_TM, t)
    const = lambda i: (0, 0)
    return pl.pallas_call(
        _ffn_ln_kernel,
        grid=(t // tm,),
        in_specs=[
            pl.BlockSpec((tm, D_MODEL), lambda i: (i, 0)),
            pl.BlockSpec((D_MODEL, 2 * D_FF), const, pipeline_mode=pl.Buffered(1)),
            pl.BlockSpec((D_FF, D_MODEL), const, pipeline_mode=pl.Buffered(1)),
            pl.BlockSpec((1, D_MODEL), const),
            pl.BlockSpec((1, D_MODEL), const),
        ],
        out_specs=pl.BlockSpec((tm, D_MODEL), lambda i: (i, 0)),
        out_shape=jax.ShapeDtypeStruct((t, D_MODEL), F32),
        compiler_params=_cparams(("parallel",)),
        name="ffn_ln",
    )(x, w_in, w_out, g.reshape(1, D_MODEL), b.reshape(1, D_MODEL))


def _proj_ssm_kernel(x_ref, w_ref, o_ref):
    xb = x_ref[...].astype(BF16)
    n = w_ref.shape[1]
    step = n // PROJ_SSM_CHUNKS
    for lo in range(0, n, step):
        o_ref[:, lo:lo + step] = _dot(xb, w_ref[:, lo:lo + step])


def _proj_ssm(x, w):
    t = x.shape[0]
    n = w.shape[1]
    tm = min(PROJ_TM, t)
    return pl.pallas_call(
        _proj_ssm_kernel,
        grid=(t // tm,),
        in_specs=[
            pl.BlockSpec((tm, D_MODEL), lambda i: (i, 0)),
            pl.BlockSpec((D_MODEL, n), lambda i: (0, 0), pipeline_mode=pl.Buffered(1)),
        ],
        out_specs=pl.BlockSpec((tm, n), lambda i: (i, 0)),
        out_shape=jax.ShapeDtypeStruct((t, n), F32),
        compiler_params=_cparams(("parallel",)),
        name="proj_ssm",
    )(x, w)


def _proj_attn_kernel(x_ref, wqk_ref, watt_ref, c_ref, s1_ref, s2_ref, qk_ref, att_ref):
    xb = x_ref[...].astype(BF16)
    for blk in range(2):
        acc = _dot(xb, wqk_ref[:, blk * A_W:(blk + 1) * A_W])
        scale = Q_SCALE if blk == 0 else 1.0
        c = c_ref[...] * scale
        s1 = s1_ref[...] * scale
        s2 = s2_ref[...] * scale
        for h in range(A_W // LANES):
            a = acc[:, h * LANES:(h + 1) * LANES]
            r = a * c + pltpu.roll(a, ROT_A // 2, axis=1) * s1 + pltpu.roll(a, LANES - ROT_A // 2, axis=1) * s2
            qk_ref[:, blk * A_W + h * LANES:blk * A_W + (h + 1) * LANES] = r.astype(qk_ref.dtype)
    for blk in range(watt_ref.shape[1] // D_MODEL):
        acc = _dot(xb, watt_ref[:, blk * D_MODEL:(blk + 1) * D_MODEL])
        if blk == A_W // D_MODEL:
            acc = acc * Q_SCALE
        att_ref[:, blk * D_MODEL:(blk + 1) * D_MODEL] = acc.astype(att_ref.dtype)


def _proj_attn(x, w_qk, w_att, tabs, seq):
    t = x.shape[0]
    tm = min(PROJ_TM, seq)
    nrow = seq // tm
    tab_spec = pl.BlockSpec((tm, LANES), lambda i: (i % nrow, 0))
    const = lambda i: (0, 0)
    nqk, natt = w_qk.shape[1], w_att.shape[1]
    return pl.pallas_call(
        _proj_attn_kernel,
        grid=(t // tm,),
        in_specs=[
            pl.BlockSpec((tm, D_MODEL), lambda i: (i, 0)),
            pl.BlockSpec((D_MODEL, nqk), const, pipeline_mode=pl.Buffered(1)),
            pl.BlockSpec((D_MODEL, natt), const, pipeline_mode=pl.Buffered(1)),
            tab_spec, tab_spec, tab_spec,
        ],
        out_specs=[pl.BlockSpec((tm, nqk), lambda i: (i, 0)), pl.BlockSpec((tm, natt), lambda i: (i, 0))],
        out_shape=[jax.ShapeDtypeStruct((t, nqk), BF16), jax.ShapeDtypeStruct((t, natt), BF16)],
        compiler_params=_cparams(("parallel",)),
        name="proj_attn",
    )(x, w_qk, w_att, *tabs)


def _rope_tables(seq):
    half = ROT_A // 2
    inv_freq = ROPE_THETA ** (-jnp.arange(half, dtype=F32) * 2.0 / ROT_A)
    ang = jnp.arange(seq, dtype=F32)[:, None] * inv_freq[None, :]
    cos, sin = jnp.cos(ang), jnp.sin(ang)
    ones = jnp.ones((seq, HD_A - ROT_A), F32)
    zeros = jnp.zeros((seq, HD_A - ROT_A), F32)
    zh = jnp.zeros((seq, half), F32)
    c = jnp.concatenate([cos, cos, ones], axis=1)
    s1 = jnp.concatenate([zh, sin, zeros], axis=1)
    s2 = jnp.concatenate([-sin, zh, zeros], axis=1)
    rep = LANES // HD_A
    return tuple(jnp.tile(v, (1, rep)) for v in (c, s1, s2))


def _diff_attn_kernel(q_ref, k_ref, v_ref, lam_ref, g_ref, o_ref, qq_ref, m_ref, l_ref, acc_ref, *, lam_init):
    t = q_ref.shape[0]
    i = pl.program_id(2)
    lane = lax.broadcasted_iota(jnp.int32, (t, LANES), 1)
    q = q_ref[...]
    zero = jnp.zeros_like(q)
    qq_ref[...] = jnp.concatenate([jnp.where(lane < HD_A, q, zero), jnp.where(lane < HD_A, zero, q)], axis=0)
    m_ref[...] = jnp.full(m_ref.shape, NEG_BIG, F32)
    l_ref[...] = jnp.zeros(l_ref.shape, F32)
    acc_ref[...] = jnp.zeros(acc_ref.shape, F32)

    def step(j, n, masked):
        m = m_ref[...]
        start = pl.multiple_of(j * t, t)
        kb = k_ref[pl.ds(start, n * t), :]
        vb = v_ref[pl.ds(start, n * t), :]
        s = _dot_nt(qq_ref[...], kb)
        if masked:
            qchunk = lax.broadcasted_iota(jnp.int32, (2 * t, t), 0) % t // CHUNK
            kchunk = lax.broadcasted_iota(jnp.int32, (2 * t, t), 1) // CHUNK
            s = jnp.where(kchunk <= qchunk, s, NEG_BIG)
        m_new = jnp.maximum(m, jnp.max(s, axis=1, keepdims=True))
        p = jnp.concatenate([jnp.exp2(s[:, c:c + LANES] - m_new) for c in range(0, n * t, LANES)], axis=1)
        corr = jnp.exp2(m - m_new)
        l_ref[...] = corr * l_ref[...] + jnp.sum(p, axis=1, keepdims=True)
        acc_ref[...] = corr * acc_ref[...] + _dot(p.astype(BF16), vb)
        m_ref[...] = m_new

    done = 0
    n = ATT_A_NT
    while n >= 1:
        cnt = (i - done) // n

        def body(g, c, n=n, done=done):
            step(done + g * n, n, False)
            return c

        lax.fori_loop(0, cnt, body, 0)
        done = done + cnt * n
        n //= 2
    step(i, 1, True)
    o = acc_ref[...] / l_ref[...]
    lf = lam_ref[...]
    lam_full = (jnp.exp(jnp.sum(lf[0:1] * lf[1:2], axis=1, keepdims=True))
                - jnp.exp(jnp.sum(lf[2:3] * lf[3:4], axis=1, keepdims=True)) + lam_init)
    d = o[:t] - lam_full * o[t:]
    y = d * lax.rsqrt(jnp.mean(d * d, axis=-1, keepdims=True) + LN_EPS) * g_ref[...]
    o_ref[...] = (y * (1.0 - lam_init)).astype(o_ref.dtype)


def _diff_attn(qk, v_arr, v_col0, lam, subln_g, layer, batch, seq):
    t = min(ATT_A_T, seq)
    lam_init = 0.8 - 0.6 * math.exp(-0.3 * layer)
    kern = functools.partial(_diff_attn_kernel, lam_init=lam_init)
    return pl.pallas_call(
        kern,
        grid=(batch, H_A, seq // t),
        in_specs=[
            pl.BlockSpec((None, t, LANES), lambda b, h, i: (b, i, h)),
            pl.BlockSpec((None, seq, LANES), lambda b, h, i: (b, 0, H_A + h)),
            pl.BlockSpec((None, seq, LANES), lambda b, h, i: (b, 0, v_col0 + h)),
            pl.BlockSpec((4, HD_A), lambda b, h, i: (0, 0)),
            pl.BlockSpec((1, 2 * HD_A), lambda b, h, i: (0, 0)),
        ],
        out_specs=pl.BlockSpec((None, t, LANES), lambda b, h, i: (b, i, h)),
        out_shape=jax.ShapeDtypeStruct((batch, seq, A_W), BF16),
        scratch_shapes=[pltpu.VMEM((2 * t, LANES), BF16)] + [pltpu.VMEM((2 * t, LANES), F32)] * 3,
        compiler_params=_cparams(("parallel", "parallel", "parallel")),
        name="diff_attn",
    )(qk, qk, v_arr, lam, subln_g.reshape(1, 2 * HD_A))


def _stick_kernel(q_ref, k_ref, v_ref, o_ref, qq_ref, run_ref, acc_ref):
    tq = q_ref.shape[0]
    tk = min(ATT_C_TK, tq)
    per = tq // tk
    i = pl.program_id(2)
    lane = lax.broadcasted_iota(jnp.int32, (tq, LANES), 1)
    q = q_ref[...]
    zero = jnp.zeros_like(q)
    qq_ref[...] = jnp.concatenate([jnp.where(lane < HD_C, q, zero), jnp.where(lane < HD_C, zero, q)], axis=0)
    u = jnp.where(lax.broadcasted_iota(jnp.int32, (tk, tk), 0) >= lax.broadcasted_iota(jnp.int32, (tk, tk), 1),
                  -1.0, 0.0).astype(BF16)

    def front(qs, j, tri):
        kb = k_ref[pl.ds(pl.multiple_of(j * tk, tk), tk), :]
        z = _dot_nt(qs, kb)
        neg_abs = lax.bitcast_convert_type(
            lax.bitcast_convert_type(z, jnp.uint32) | jnp.uint32(0x80000000), F32)
        sp = jnp.maximum(z, 0.0) + jnp.log2(1.0 + jnp.exp2(neg_abs))
        if tri is not None:
            sp = jnp.where(tri, sp, 0.0)
        tail = _dot(sp.astype(BF16), u)
        return z + tail, tail[:, 0:1]

    def back(j, w, run, tri):
        a = jnp.concatenate([jnp.exp2(w[:, c:c + LANES] + run) for c in range(0, tk, LANES)], axis=1)
        if tri is not None:
            a = jnp.where(tri, a, 0.0)
        return _dot(a.astype(BF16), v_ref[pl.ds(pl.multiple_of(j * tk, tk), tk), :])

    def group(qs, js, tris, run, acc):
        fronts = [front(qs, j, tri) for j, tri in zip(js, tris)]
        for j, tri, (w, mass) in zip(js, tris, fronts):
            acc = acc + back(j, w, run, tri)
            run = run + mass
        return run, acc

    def full_group(first, n):
        run, acc = group(qq_ref[...], [first - m for m in range(n)], [None] * n, run_ref[...], acc_ref[...])
        run_ref[...] = run
        acc_ref[...] = acc

    tri = (lax.broadcasted_iota(jnp.int32, (2 * tk, tk), 1)
           < lax.broadcasted_iota(jnp.int32, (2 * tk, tk), 0) % tk)
    zeros = jnp.zeros((2 * tk, LANES), F32)
    for r in range(per):
        even, odd = pl.ds(r * tk, tk), pl.ds(tq + r * tk, tk)
        qs = jnp.concatenate([qq_ref[even, :], qq_ref[odd, :]], axis=0)
        run, acc = group(qs, [per * i + r - m for m in range(r + 1)], [tri] + [None] * r, zeros, zeros)
        run_ref[even, :], run_ref[odd, :] = run[:tk], run[tk:]
        acc_ref[even, :], acc_ref[odd, :] = acc[:tk], acc[tk:]

    nfull = per * i
    ngrp = nfull // ATT_C_NT
    nrest = (nfull - ATT_C_NT * ngrp) // per

    def grp_body(n, c):
        full_group(nfull - 1 - ATT_C_NT * n, ATT_C_NT)
        return c

    def rest_body(n, c):
        full_group(per * (nrest - n) - 1, per)
        return c

    lax.fori_loop(0, ngrp, grp_body, 0)
    lax.fori_loop(0, nrest, rest_body, 0)
    o_ref[...] = jnp.where(lane < HD_C, acc_ref[0:tq, :], acc_ref[tq:2 * tq, :]).astype(o_ref.dtype)


def _stick_breaking(arr, q_col0, k_col0, v_col0, batch, seq):
    t = min(ATT_C_T, seq)
    nhp = C_W // LANES
    return pl.pallas_call(
        _stick_kernel,
        grid=(batch, nhp, seq // t),
        in_specs=[
            pl.BlockSpec((None, t, LANES), lambda b, h, i: (b, i, q_col0 + h)),
            pl.BlockSpec((None, seq, LANES), lambda b, h, i: (b, 0, k_col0 + h)),
            pl.BlockSpec((None, seq, LANES), lambda b, h, i: (b, 0, v_col0 + h)),
        ],
        out_specs=pl.BlockSpec((None, t, LANES), lambda b, h, i: (b, i, h)),
        out_shape=jax.ShapeDtypeStruct((batch, seq, C_W), BF16),
        scratch_shapes=[pltpu.VMEM((2 * t, LANES), BF16)] + [pltpu.VMEM((2 * t, LANES), F32)] * 2,
        compiler_params=_cparams(("parallel", "parallel", "parallel")),
        name="stick_breaking",
    )(arr, arr, arr)


HALO = 8


def _ssd_kernel(z_ref, x_ref, bc_ref, dt_ref, cw_ref, cb_ref, dtb_ref, alog_ref, dskip_ref, ng_ref,
                o_ref, xe_ref, st_ref):
    L = z_ref.shape[0]
    c = pl.program_id(1)

    @pl.when(c == 0)
    def _():
        xe_ref[0:HALO, :] = jnp.zeros((HALO, CONV_DIM), F32)
        st_ref[...] = jnp.zeros(st_ref.shape, F32)

    xe_ref[HALO:HALO + L, 0:D_INNER] = x_ref[...]
    xe_ref[HALO:HALO + L, D_INNER:CONV_DIM] = bc_ref[...]
    conv = cb_ref[...]
    for r in range(CONV_B):
        off = HALO - (CONV_B - 1) + r
        conv = conv + cw_ref[r:r + 1, :] * xe_ref[off:off + L, :]
    xe_ref[0:HALO, :] = xe_ref[L:L + HALO, :]
    xc = conv * jax.nn.sigmoid(conv)
    xs = xc[:, :D_INNER]

    dt = _softplus(dt_ref[...] + dtb_ref[...])
    a = dt * (-jnp.exp(alog_ref[...]))
    row_i = lax.broadcasted_iota(jnp.int32, (L, L), 0)
    col_i = lax.broadcasted_iota(jnp.int32, (L, L), 1)
    causal = col_i <= row_i
    tri = jnp.where(causal, 1.0, 0.0).astype(BF16)
    acum = _dot_exact_rhs(tri, a)
    acum_t = acum.T
    a_last = acum[L - 1:L, :]
    e_in = jnp.exp(acum)
    e_out = jnp.exp(a_last - acum)

    expand = jnp.where(lax.broadcasted_iota(jnp.int32, (LANES, D_INNER), 0)
                       == lax.broadcasted_iota(jnp.int32, (LANES, D_INNER), 1) // P_B,
                       1.0, 0.0).astype(BF16)
    dt_x = _dot_exact_lhs(dt, expand)
    e_in_x = _dot_exact_lhs(e_in, expand)
    e_out_x = _dot_exact_lhs(e_out, expand)

    xdt = xs * dt_x
    xdt_b = xdt.astype(BF16)
    xdt_out_b = (xdt * e_out_x).astype(BF16)
    lane = lax.broadcasted_iota(jnp.int32, (L, LANES), 1)

    hg = H_B // G_B
    gw = D_INNER // G_B
    y_parts = []
    for g in range(G_B):
        bm_f = xc[:, D_INNER + g * N_B:D_INNER + (g + 1) * N_B]
        bm = bm_f.astype(BF16)
        bm_t = bm_f.T.astype(BF16)
        cm = xc[:, D_INNER + G_B * N_B + g * N_B:D_INNER + G_B * N_B + (g + 1) * N_B].astype(BF16)
        cb = _dot_nt(cm, bm)
        st_g = st_ref[:, g * gw:(g + 1) * gw]
        y_off = _dot(cm, st_g.astype(BF16)) * e_in_x[:, g * gw:(g + 1) * gw]
        y_diag = []
        for hp in range(hg // 2):
            h0 = g * hg + 2 * hp
            pair = xdt_b[:, h0 * P_B:(h0 + 2) * P_B]
            ys = []
            for h in (h0, h0 + 1):
                seg = acum[:, h:h + 1] - acum_t[h:h + 1, :]
                m = (cb * jnp.exp(jnp.where(causal, seg, -jnp.inf))).astype(BF16)
                ys.append(_dot(m, pair))
            y_diag.append(jnp.where(lane < P_B, ys[0], ys[1]))
        y_parts.append(jnp.concatenate(y_diag, axis=1) + y_off)
        upd = _dot(bm_t, xdt_out_b[:, g * gw:(g + 1) * gw])
        st_ref[:, g * gw:(g + 1) * gw] = st_g * e_in_x[L - 1:L, g * gw:(g + 1) * gw] + upd

    y = jnp.concatenate(y_parts, axis=1) + dskip_ref[...] * xs
    zz = z_ref[...]
    y = y * (zz * jax.nn.sigmoid(zz))
    outs = []
    for g in range(G_B):
        yg = y[:, g * gw:(g + 1) * gw]
        outs.append(yg * lax.rsqrt(jnp.mean(yg * yg, axis=-1, keepdims=True) + LN_EPS))
    o_ref[...] = (jnp.concatenate(outs, axis=1) * ng_ref[...]).astype(o_ref.dtype)


def _ssd(ssm, conv_w, conv_b, dt_bias, a_log, d_skip, norm_g, batch, seq):
    L = min(SSD_L, seq)
    pad = LANES - H_B
    row = lambda v: v.reshape(1, -1).astype(F32)
    dtb = jnp.pad(row(dt_bias), ((0, 0), (0, pad)))
    alog = jnp.pad(row(a_log), ((0, 0), (0, pad)))
    dskip = jnp.repeat(row(d_skip), P_B, axis=1)
    const = lambda b, c: (0, 0)
    bc_w = CONV_DIM - D_INNER
    return pl.pallas_call(
        _ssd_kernel,
        grid=(batch, seq // L),
        in_specs=[
            pl.BlockSpec((None, L, D_INNER), lambda b, c: (b, c, 0)),
            pl.BlockSpec((None, L, D_INNER), lambda b, c: (b, c, 1)),
            pl.BlockSpec((None, L, bc_w), lambda b, c: (b, c, 2 * D_INNER // bc_w)),
            pl.BlockSpec((None, L, LANES), lambda b, c: (b, c, (D_INNER + CONV_DIM) // LANES)),
            pl.BlockSpec((CONV_B, CONV_DIM), const),
            pl.BlockSpec((1, CONV_DIM), const),
            pl.BlockSpec((1, LANES), const),
            pl.BlockSpec((1, LANES), const),
            pl.BlockSpec((1, D_INNER), const),
            pl.BlockSpec((1, D_INNER), const),
        ],
        out_specs=pl.BlockSpec((None, L, D_INNER), lambda b, c: (b, c, 0)),
        out_shape=jax.ShapeDtypeStruct((batch, seq, D_INNER), BF16),
        scratch_shapes=[pltpu.VMEM((HALO + L, CONV_DIM), F32), pltpu.VMEM((N_B, D_INNER), F32)],
        compiler_params=_cparams(("parallel", "arbitrary")),
        name="ssd",
    )(ssm, ssm, ssm, ssm, conv_w, row(conv_b), dtb, alog, dskip, row(norm_g))


def _merge_ln_kernel(h_ref, oa_ref, ob_ref, oc_ref, wg_ref, gb_ref, wbr_ref, wout_ref, g_ref, b_ref, o_ref):
    h = h_ref[...]
    hb = h.astype(BF16)
    merged = jnp.zeros(h.shape, F32)
    for r, br_ref in enumerate((oa_ref, ob_ref, oc_ref)):
        gpre = _dot(hb, wg_ref[:, r * D_MODEL:(r + 1) * D_MODEL]) + gb_ref[r:r + 1, :]
        merged = merged + jax.nn.sigmoid(gpre) * _dot(br_ref[...], wbr_ref[r])
    mix = _dot(merged.astype(BF16), wout_ref[...])
    o_ref[...] = _layernorm(DEEPNORM_ALPHA * h + mix, g_ref[...], b_ref[...])


def _merge_ln(h, oa, ob, oc, w_gate, gate_bias, w_branch, w_out, g, b):
    t = h.shape[0]
    tm = min(MERGE_TM, t)
    rowspec = pl.BlockSpec((tm, D_MODEL), lambda i: (i, 0))
    const2 = lambda i: (0, 0)
    one = pl.Buffered(1)
    return pl.pallas_call(
        _merge_ln_kernel,
        grid=(t // tm,),
        in_specs=[
            rowspec, rowspec, rowspec, rowspec,
            pl.BlockSpec((D_MODEL, N_BRANCH * D_MODEL), const2, pipeline_mode=one),
            pl.BlockSpec((N_BRANCH, D_MODEL), const2),
            pl.BlockSpec((N_BRANCH, D_MODEL, D_MODEL), lambda i: (0, 0, 0), pipeline_mode=one),
            pl.BlockSpec((D_MODEL, D_MODEL), const2, pipeline_mode=one),
            pl.BlockSpec((1, D_MODEL), const2),
            pl.BlockSpec((1, D_MODEL), const2),
        ],
        out_specs=rowspec,
        out_shape=jax.ShapeDtypeStruct((t, D_MODEL), F32),
        compiler_params=_cparams(("parallel",)),
        name="merge_ln",
    )(h, oa, ob, oc, w_gate, gate_bias, w_branch, w_out, g.reshape(1, D_MODEL), b.reshape(1, D_MODEL))


def _mixer_weights(w_in):
    segs = (A_W, A_W, A_W, D_INNER, CONV_DIM, H_B, C_W, C_W, C_W, N_BRANCH * D_MODEL)
    offs = [0]
    for s in segs:
        offs.append(offs[-1] + s)
    qa, ka, va, zb, xbc, dtb, qc, kc, vc, gate = (w_in[:, offs[n]:offs[n + 1]] for n in range(len(segs)))
    w_qk = jnp.concatenate([qa, ka], axis=1).astype(BF16)
    w_att = jnp.concatenate([va, qc, kc, vc], axis=1).astype(BF16)
    w_ssm = jnp.concatenate([zb, xbc, jnp.pad(dtb, ((0, 0), (0, LANES - H_B)))], axis=1).astype(BF16)
    return w_qk, w_att, w_ssm, gate.astype(BF16)


def kernel(x, ffn1_w_in, ffn1_w_out, ln1_g, ln1_b, w_mix_in, gate_bias, diff_lambda, diff_subln_g,
           ssm_conv_w, ssm_conv_b, ssm_dt_bias, ssm_A_log, ssm_D, ssm_norm_g, w_branch, w_mix_out,
           ln2_g, ln2_b, ffn2_w_in, ffn2_w_out, ln3_g, ln3_b):
    batch, seq, _ = x.shape
    t = batch * seq
    tabs = _rope_tables(seq)
    xf = x.reshape(t, D_MODEL)
    for l in range(DEPTH):
        xf = _ffn_ln(xf, ffn1_w_in[l].astype(BF16), ffn1_w_out[l].astype(BF16), ln1_g[l], ln1_b[l])

        w_qk, w_att, w_ssm, w_gate = _mixer_weights(w_mix_in[l])
        qk, att = _proj_attn(xf, w_qk, w_att, tabs, seq)
        qk = qk.reshape(batch, seq, 2 * A_W)
        att = att.reshape(batch, seq, A_W + 3 * C_W)
        ssm = _proj_ssm(xf, w_ssm).reshape(batch, seq, w_ssm.shape[1])

        o_a = _diff_attn(qk, att, 0, diff_lambda[l], diff_subln_g[l], l, batch, seq)
        nb = A_W // LANES
        o_c = _stick_breaking(att, nb, 2 * nb, 3 * nb, batch, seq)
        o_b = _ssd(ssm, ssm_conv_w[l], ssm_conv_b[l], ssm_dt_bias[l], ssm_A_log[l], ssm_D[l],
                   ssm_norm_g[l], batch, seq)

        xf = _merge_ln(xf, o_a.reshape(t, A_W), o_b.reshape(t, D_INNER), o_c.reshape(t, C_W),
                       w_gate, gate_bias[l], w_branch[l].astype(BF16), w_mix_out[l].astype(BF16),
                       ln2_g[l], ln2_b[l])

        xf = _ffn_ln(xf, ffn2_w_in[l].astype(BF16), ffn2_w_out[l].astype(BF16), ln3_g[l], ln3_b[l])
    return xf.reshape(batch, seq, D_MODEL)
```

```python
import functools
import math

import jax
import jax.numpy as jnp
from jax import lax
from jax.experimental import pallas as pl
from jax.experimental.pallas import tpu as pltpu

F32 = jnp.float32
BF16 = jnp.bfloat16

D_MODEL = 1024
DEPTH = 2
LN_EPS = 1e-5
DEEPNORM_ALPHA = (2 * DEPTH) ** 0.25
D_FF = 2816
FFN_RES = 0.5

CHUNK = 64
H_A = 8
HD_A = 64
ROT_A = HD_A // 4
ROPE_THETA = 500000.0
A_W = H_A * 2 * HD_A

H_B = 16
P_B = 64
D_INNER = H_B * P_B
G_B = 2
N_B = 128
CONV_B = 4
CONV_DIM = D_INNER + 2 * G_B * N_B

H_C = 16
HD_C = 64
C_W = H_C * HD_C
N_BRANCH = 3

LANES = 128
VMEM_LIMIT = 56 * 1024 * 1024

FFN_TM = 512
FFN_CHUNK = 256
PROJ_TM = 512
PROJ_SSM_CHUNKS = 3
MERGE_TM = 512
ATT_A_T = 512
ATT_A_NT = 4
ATT_C_T = 512
ATT_C_TK = 256
ATT_C_NT = 4
SSD_L = 256

assert D_FF % FFN_CHUNK == 0
assert ATT_C_T % ATT_C_TK == 0 and ATT_C_NT % (ATT_C_T // ATT_C_TK) == 0
assert HD_A == HD_C

NEG_BIG = -1e30
LOG2E = 1.4426950408889634
Q_SCALE = HD_A ** -0.5 * LOG2E


def _cparams(sem):
    return pltpu.CompilerParams(dimension_semantics=sem, vmem_limit_bytes=VMEM_LIMIT)


def _dot(a, b):
    return jnp.dot(a, b, preferred_element_type=F32)


def _dot_nt(a, b):
    return lax.dot_general(a, b, (((1,), (1,)), ((), ())), preferred_element_type=F32)


def _split3(x):
    hi = x.astype(BF16)
    r = x - hi.astype(F32)
    mid = r.astype(BF16)
    lo = (r - mid.astype(F32)).astype(BF16)
    return hi, mid, lo


def _dot_exact_rhs(a_bf16, x):
    hi, mid, lo = _split3(x)
    return _dot(a_bf16, hi) + (_dot(a_bf16, mid) + _dot(a_bf16, lo))


def _dot_exact_lhs(x, a_bf16):
    hi, mid, lo = _split3(x)
    return _dot(hi, a_bf16) + (_dot(mid, a_bf16) + _dot(lo, a_bf16))


def _layernorm(y, g, b):
    mu = jnp.mean(y, axis=-1, keepdims=True)
    yc = y - mu
    var = jnp.mean(yc * yc, axis=-1, keepdims=True)
    return yc * lax.rsqrt(var + LN_EPS) * g + b


def _softplus(x):
    return jnp.maximum(x, 0.0) + jnp.log(1.0 + jnp.exp(-jnp.abs(x)))


def _ffn_ln_kernel(x_ref, win_ref, wout_ref, g_ref, b_ref, o_ref):
    x = x_ref[...]
    xb = x.astype(BF16)
    acc = jnp.zeros(x.shape, F32)
    for c in range(D_FF // FFN_CHUNK):
        lo = c * FFN_CHUNK
        gate = _dot(xb, win_ref[:, lo:lo + FFN_CHUNK])
        up = _dot(xb, win_ref[:, D_FF + lo:D_FF + lo + FFN_CHUNK])
        a = (gate * jax.nn.sigmoid(gate) * up).astype(BF16)
        acc = acc + _dot(a, wout_ref[lo:lo + FFN_CHUNK, :])
    y = DEEPNORM_ALPHA * x + FFN_RES * acc
    o_ref[...] = _layernorm(y, g_ref[...], b_ref[...])


def _ffn_ln(x, w_in, w_out, g, b):
    t = x.shape[0]
    tm = min(FFN_TM, t)
    const = lambda i: (0, 0)
    return pl.pallas_call(
        _ffn_ln_kernel,
        grid=(t // tm,),
        in_specs=[
            pl.BlockSpec((tm, D_MODEL), lambda i: (i, 0)),
            pl.BlockSpec((D_MODEL, 2 * D_FF), const, pipeline_mode=pl.Buffered(1)),
            pl.BlockSpec((D_FF, D_MODEL), const, pipeline_mode=pl.Buffered(1)),
            pl.BlockSpec((1, D_MODEL), const),
            pl.BlockSpec((1, D_MODEL), const),
        ],
        out_specs=pl.BlockSpec((tm, D_MODEL), lambda i: (i, 0)),
        out_shape=jax.ShapeDtypeStruct((t, D_MODEL), F32),
        compiler_params=_cparams(("parallel",)),
        name="ffn_ln",
    )(x, w_in, w_out, g.reshape(1, D_MODEL), b.reshape(1, D_MODEL))


def _proj_ssm_kernel(x_ref, w_ref, o_ref):
    xb = x_ref[...].astype(BF16)
    n = w_ref.shape[1]
    step = n // PROJ_SSM_CHUNKS
    for lo in range(0, n, step):
        o_ref[:, lo:lo + step] = _dot(xb, w_ref[:, lo:lo + step])


def _proj_ssm(x, w):
    t = x.shape[0]
    n = w.shape[1]
    tm = min(PROJ_TM, t)
    return pl.pallas_call(
        _proj_ssm_kernel,
        grid=(t // tm,),
        in_specs=[
            pl.BlockSpec((tm, D_MODEL), lambda i: (i, 0)),
            pl.BlockSpec((D_MODEL, n), lambda i: (0, 0), pipeline_mode=pl.Buffered(1)),
        ],
        out_specs=pl.BlockSpec((tm, n), lambda i: (i, 0)),
        out_shape=jax.ShapeDtypeStruct((t, n), F32),
        compiler_params=_cparams(("parallel",)),
        name="proj_ssm",
    )(x, w)


def _proj_attn_kernel(x_ref, wqk_ref, watt_ref, c_ref, s1_ref, s2_ref, qk_ref, att_ref):
    xb = x_ref[...].astype(BF16)
    for blk in range(2):
        acc = _dot(xb, wqk_ref[:, blk * A_W:(blk + 1) * A_W])
        scale = Q_SCALE if blk == 0 else 1.0
        c = c_ref[...] * scale
        s1 = s1_ref[...] * scale
        s2 = s2_ref[...] * scale
        for h in range(A_W // LANES):
            a = acc[:, h * LANES:(h + 1) * LANES]
            r = a * c + pltpu.roll(a, ROT_A // 2, axis=1) * s1 + pltpu.roll(a, LANES - ROT_A // 2, axis=1) * s2
            qk_ref[:, blk * A_W + h * LANES:blk * A_W + (h + 1) * LANES] = r.astype(qk_ref.dtype)
    for blk in range(watt_ref.shape[1] // D_MODEL):
        acc = _dot(xb, watt_ref[:, blk * D_MODEL:(blk + 1) * D_MODEL])
        if blk == A_W // D_MODEL:
            acc = acc * Q_SCALE
        att_ref[:, blk * D_MODEL:(blk + 1) * D_MODEL] = acc.astype(att_ref.dtype)


def _proj_attn(x, w_qk, w_att, tabs, seq):
    t = x.shape[0]
    tm = min(PROJ_TM, seq)
    nrow = seq // tm
    tab_spec = pl.BlockSpec((tm, LANES), lambda i: (i % nrow, 0))
    const = lambda i: (0, 0)
    nqk, natt = w_qk.shape[1], w_att.shape[1]
    return pl.pallas_call(
        _proj_attn_kernel,
        grid=(t // tm,),
        in_specs=[
            pl.BlockSpec((tm, D_MODEL), lambda i: (i, 0)),
            pl.BlockSpec((D_MODEL, nqk), const, pipeline_mode=pl.Buffered(1)),
            pl.BlockSpec((D_MODEL, natt), const, pipeline_mode=pl.Buffered(1)),
            tab_spec, tab_spec, tab_spec,
        ],
        out_specs=[pl.BlockSpec((tm, nqk), lambda i: (i, 0)), pl.BlockSpec((tm, natt), lambda i: (i, 0))],
        out_shape=[jax.ShapeDtypeStruct((t, nqk), BF16), jax.ShapeDtypeStruct((t, natt), BF16)],
        compiler_params=_cparams(("parallel",)),
        name="proj_attn",
    )(x, w_qk, w_att, *tabs)


def _rope_tables(seq):
    half = ROT_A // 2
    inv_freq = ROPE_THETA ** (-jnp.arange(half, dtype=F32) * 2.0 / ROT_A)
    ang = jnp.arange(seq, dtype=F32)[:, None] * inv_freq[None, :]
    cos, sin = jnp.cos(ang), jnp.sin(ang)
    ones = jnp.ones((seq, HD_A - ROT_A), F32)
    zeros = jnp.zeros((seq, HD_A - ROT_A), F32)
    zh = jnp.zeros((seq, half), F32)
    c = jnp.concatenate([cos, cos, ones], axis=1)
    s1 = jnp.concatenate([zh, sin, zeros], axis=1)
    s2 = jnp.concatenate([-sin, zh, zeros], axis=1)
    rep = LANES // HD_A
    return tuple(jnp.tile(v, (1, rep)) for v in (c, s1, s2))


def _diff_attn_kernel(q_ref, k_ref, v_ref, lam_ref, g_ref, o_ref, qq_ref, m_ref, l_ref, acc_ref, *, lam_init):
    t = q_ref.shape[0]
    i = pl.program_id(2)
    lane = lax.broadcasted_iota(jnp.int32, (t, LANES), 1)
    q = q_ref[...]
    zero = jnp.zeros_like(q)
    qq_ref[...] = jnp.concatenate([jnp.where(lane < HD_A, q, zero), jnp.where(lane < HD_A, zero, q)], axis=0)
    m_ref[...] = jnp.full(m_ref.shape, NEG_BIG, F32)
    l_ref[...] = jnp.zeros(l_ref.shape, F32)
    acc_ref[...] = jnp.zeros(acc_ref.shape, F32)

    def step(j, n, masked):
        start = pl.multiple_of(j * t, t)
        kb = k_ref[pl.ds(start, n * t), :]
        vb = v_ref[pl.ds(start, n * t), :]
        nrows = t if n == ATT_A_NT else 2 * t
        for r0 in range(0, 2 * t, nrows):
            rows = pl.ds(r0, nrows)
            m = m_ref[rows, :]
            s = _dot_nt(qq_ref[rows, :], kb)
            if masked:
                qchunk = lax.broadcasted_iota(jnp.int32, (nrows, t), 0) % t // CHUNK
                kchunk = lax.broadcasted_iota(jnp.int32, (nrows, t), 1) // CHUNK
                s = jnp.where(kchunk <= qchunk, s, NEG_BIG)
            m_new = jnp.maximum(m, jnp.max(s, axis=1, keepdims=True))
            p = jnp.concatenate([jnp.exp2(s[:, c:c + LANES] - m_new) for c in range(0, n * t, LANES)], axis=1)
            corr = jnp.exp2(m - m_new)
            l_ref[rows, :] = corr * l_ref[rows, :] + jnp.sum(p, axis=1, keepdims=True)
            acc_ref[rows, :] = corr * acc_ref[rows, :] + _dot(p.astype(BF16), vb)
            m_ref[rows, :] = m_new

    done = 0
    n = ATT_A_NT
    while n >= 1:
        cnt = (i - done) // n

        def body(g, c, n=n, done=done):
            step(done + g * n, n, False)
            return c

        lax.fori_loop(0, cnt, body, 0)
        done = done + cnt * n
        n //= 2
    step(i, 1, True)
    o = acc_ref[...] / l_ref[...]
    lf = lam_ref[...]
    lam_full = (jnp.exp(jnp.sum(lf[0:1] * lf[1:2], axis=1, keepdims=True))
                - jnp.exp(jnp.sum(lf[2:3] * lf[3:4], axis=1, keepdims=True)) + lam_init)
    d = o[:t] - lam_full * o[t:]
    y = d * lax.rsqrt(jnp.mean(d * d, axis=-1, keepdims=True) + LN_EPS) * g_ref[...]
    o_ref[...] = (y * (1.0 - lam_init)).astype(o_ref.dtype)


def _diff_attn(qk, v_arr, v_col0, lam, subln_g, layer, batch, seq):
    t = min(ATT_A_T, seq)
    lam_init = 0.8 - 0.6 * math.exp(-0.3 * layer)
    kern = functools.partial(_diff_attn_kernel, lam_init=lam_init)
    return pl.pallas_call(
        kern,
        grid=(batch, H_A, seq // t),
        in_specs=[
            pl.BlockSpec((None, t, LANES), lambda b, h, i: (b, i, h)),
            pl.BlockSpec((None, seq, LANES), lambda b, h, i: (b, 0, H_A + h)),
            pl.BlockSpec((None, seq, LANES), lambda b, h, i: (b, 0, v_col0 + h)),
            pl.BlockSpec((4, HD_A), lambda b, h, i: (0, 0)),
            pl.BlockSpec((1, 2 * HD_A), lambda b, h, i: (0, 0)),
        ],
        out_specs=pl.BlockSpec((None, t, LANES), lambda b, h, i: (b, i, h)),
        out_shape=jax.ShapeDtypeStruct((batch, seq, A_W), BF16),
        scratch_shapes=[pltpu.VMEM((2 * t, LANES), BF16)] + [pltpu.VMEM((2 * t, LANES), F32)] * 3,
        compiler_params=_cparams(("parallel", "parallel", "parallel")),
        name="diff_attn",
    )(qk, qk, v_arr, lam, subln_g.reshape(1, 2 * HD_A))


def _stick_kernel(q_ref, k_ref, v_ref, o_ref, qq_ref, run_ref, acc_ref):
    tq = q_ref.shape[0]
    tk = min(ATT_C_TK, tq)
    per = tq // tk
    i = pl.program_id(2)
    lane = lax.broadcasted_iota(jnp.int32, (tq, LANES), 1)
    q = q_ref[...]
    zero = jnp.zeros_like(q)
    qq_ref[...] = jnp.concatenate([jnp.where(lane < HD_C, q, zero), jnp.where(lane < HD_C, zero, q)], axis=0)
    u = jnp.where(lax.broadcasted_iota(jnp.int32, (tk, tk), 0) >= lax.broadcasted_iota(jnp.int32, (tk, tk), 1),
                  -1.0, 0.0).astype(BF16)

    def front(qs, j, tri):
        kb = k_ref[pl.ds(pl.multiple_of(j * tk, tk), tk), :]
        z = _dot_nt(qs, kb)
        neg_abs = lax.bitcast_convert_type(
            lax.bitcast_convert_type(z, jnp.uint32) | jnp.uint32(0x80000000), F32)
        sp = jnp.maximum(z, 0.0) + jnp.log2(1.0 + jnp.exp2(neg_abs))
        if tri is not None:
            sp = jnp.where(tri, sp, 0.0)
        tail = _dot(sp.astype(BF16), u)
        return z + tail, tail[:, 0:1]

    def back(j, w, run, tri):
        a = jnp.concatenate([jnp.exp2(w[:, c:c + LANES] + run) for c in range(0, tk, LANES)], axis=1)
        if tri is not None:
            a = jnp.where(tri, a, 0.0)
        return _dot(a.astype(BF16), v_ref[pl.ds(pl.multiple_of(j * tk, tk), tk), :])

    def group(qs, js, tris, run, acc):
        fronts = [front(qs, j, tri) for j, tri in zip(js, tris)]
        for j, tri, (w, mass) in zip(js, tris, fronts):
            acc = acc + back(j, w, run, tri)
            run = run + mass
        return run, acc

    def full_group(first, n):
        run, acc = group(qq_ref[...], [first - m for m in range(n)], [None] * n, run_ref[...], acc_ref[...])
        run_ref[...] = run
        acc_ref[...] = acc

    tri = (lax.broadcasted_iota(jnp.int32, (2 * tk, tk), 1)
           < lax.broadcasted_iota(jnp.int32, (2 * tk, tk), 0) % tk)
    zeros = jnp.zeros((2 * tk, LANES), F32)
    for r in range(per):
        even, odd = pl.ds(r * tk, tk), pl.ds(tq + r * tk, tk)
        qs = jnp.concatenate([qq_ref[even, :], qq_ref[odd, :]], axis=0)
        run, acc = group(qs, [per * i + r - m for m in range(r + 1)], [tri] + [None] * r, zeros, zeros)
        run_ref[even, :], run_ref[odd, :] = run[:tk], run[tk:]
        acc_ref[even, :], acc_ref[odd, :] = acc[:tk], acc[tk:]

    nfull = per * i
    ngrp = nfull // ATT_C_NT
    nrest = (nfull - ATT_C_NT * ngrp) // per

    def grp_body(n, c):
        full_group(nfull - 1 - ATT_C_NT * n, ATT_C_NT)
        return c

    def rest_body(n, c):
        full_group(per * (nrest - n) - 1, per)
        return c

    lax.fori_loop(0, ngrp, grp_body, 0)
    lax.fori_loop(0, nrest, rest_body, 0)
    o_ref[...] = jnp.where(lane < HD_C, acc_ref[0:tq, :], acc_ref[tq:2 * tq, :]).astype(o_ref.dtype)


def _stick_breaking(arr, q_col0, k_col0, v_col0, batch, seq):
    t = min(ATT_C_T, seq)
    nhp = C_W // LANES
    return pl.pallas_call(
        _stick_kernel,
        grid=(batch, nhp, seq // t),
        in_specs=[
            pl.BlockSpec((None, t, LANES), lambda b, h, i: (b, i, q_col0 + h)),
            pl.BlockSpec((None, seq, LANES), lambda b, h, i: (b, 0, k_col0 + h)),
            pl.BlockSpec((None, seq, LANES), lambda b, h, i: (b, 0, v_col0 + h)),
        ],
        out_specs=pl.BlockSpec((None, t, LANES), lambda b, h, i: (b, i, h)),
        out_shape=jax.ShapeDtypeStruct((batch, seq, C_W), BF16),
        scratch_shapes=[pltpu.VMEM((2 * t, LANES), BF16)] + [pltpu.VMEM((2 * t, LANES), F32)] * 2,
        compiler_params=_cparams(("parallel", "parallel", "parallel")),
        name="stick_breaking",
    )(arr, arr, arr)


HALO = 8


def _ssd_kernel(z_ref, x_ref, bc_ref, dt_ref, cw_ref, cb_ref, dtb_ref, alog_ref, dskip_ref, ng_ref,
                o_ref, xe_ref, st_ref):
    L = z_ref.shape[0]
    c = pl.program_id(1)

    @pl.when(c == 0)
    def _():
        xe_ref[0:HALO, :] = jnp.zeros((HALO, CONV_DIM), F32)
        st_ref[...] = jnp.zeros(st_ref.shape, F32)

    xe_ref[HALO:HALO + L, 0:D_INNER] = x_ref[...]
    xe_ref[HALO:HALO + L, D_INNER:CONV_DIM] = bc_ref[...]
    conv = cb_ref[...]
    for r in range(CONV_B):
        off = HALO - (CONV_B - 1) + r
        conv = conv + cw_ref[r:r + 1, :] * xe_ref[off:off + L, :]
    xe_ref[0:HALO, :] = xe_ref[L:L + HALO, :]
    xc = conv * jax.nn.sigmoid(conv)
    xs = xc[:, :D_INNER]

    dt = _softplus(dt_ref[...] + dtb_ref[...])
    a = dt * (-jnp.exp(alog_ref[...]))
    row_i = lax.broadcasted_iota(jnp.int32, (L, L), 0)
    col_i = lax.broadcasted_iota(jnp.int32, (L, L), 1)
    causal = col_i <= row_i
    tri = jnp.where(causal, 1.0, 0.0).astype(BF16)
    acum = _dot_exact_rhs(tri, a)
    acum_t = acum.T
    a_last = acum[L - 1:L, :]
    e_in = jnp.exp(acum)
    e_out = jnp.exp(a_last - acum)

    expand = jnp.where(lax.broadcasted_iota(jnp.int32, (LANES, D_INNER), 0)
                       == lax.broadcasted_iota(jnp.int32, (LANES, D_INNER), 1) // P_B,
                       1.0, 0.0).astype(BF16)
    dt_x = _dot_exact_lhs(dt, expand)
    e_in_x = _dot_exact_lhs(e_in, expand)
    e_out_x = _dot_exact_lhs(e_out, expand)

    xdt = xs * dt_x
    xdt_b = xdt.astype(BF16)
    xdt_out_b = (xdt * e_out_x).astype(BF16)
    lane = lax.broadcasted_iota(jnp.int32, (L, LANES), 1)

    hg = H_B // G_B
    gw = D_INNER // G_B
    y_parts = []
    for g in range(G_B):
        bm_f = xc[:, D_INNER + g * N_B:D_INNER + (g + 1) * N_B]
        bm = bm_f.astype(BF16)
        bm_t = bm_f.T.astype(BF16)
        cm = xc[:, D_INNER + G_B * N_B + g * N_B:D_INNER + G_B * N_B + (g + 1) * N_B].astype(BF16)
        cb = _dot_nt(cm, bm)
        st_g = st_ref[:, g * gw:(g + 1) * gw]
        y_off = _dot(cm, st_g.astype(BF16)) * e_in_x[:, g * gw:(g + 1) * gw]
        y_diag = []
        for hp in range(hg // 2):
            h0 = g * hg + 2 * hp
            pair = xdt_b[:, h0 * P_B:(h0 + 2) * P_B]
            ys = []
            for h in (h0, h0 + 1):
                seg = acum[:, h:h + 1] - acum_t[h:h + 1, :]
                m = (cb * jnp.exp(jnp.where(causal, seg, -jnp.inf))).astype(BF16)
                ys.append(_dot(m, pair))
            y_diag.append(jnp.where(lane < P_B, ys[0], ys[1]))
        y_parts.append(jnp.concatenate(y_diag, axis=1) + y_off)
        upd = _dot(bm_t, xdt_out_b[:, g * gw:(g + 1) * gw])
        st_ref[:, g * gw:(g + 1) * gw] = st_g * e_in_x[L - 1:L, g * gw:(g + 1) * gw] + upd

    y = jnp.concatenate(y_parts, axis=1) + dskip_ref[...] * xs
    zz = z_ref[...]
    y = y * (zz * jax.nn.sigmoid(zz))
    outs = []
    for g in range(G_B):
        yg = y[:, g * gw:(g + 1) * gw]
        outs.append(yg * lax.rsqrt(jnp.mean(yg * yg, axis=-1, keepdims=True) + LN_EPS))
    o_ref[...] = (jnp.concatenate(outs, axis=1) * ng_ref[...]).astype(o_ref.dtype)


def _ssd(ssm, conv_w, conv_b, dt_bias, a_log, d_skip, norm_g, batch, seq):
    L = min(SSD_L, seq)
    pad = LANES - H_B
    row = lambda v: v.reshape(1, -1).astype(F32)
    dtb = jnp.pad(row(dt_bias), ((0, 0), (0, pad)))
    alog = jnp.pad(row(a_log), ((0, 0), (0, pad)))
    dskip = jnp.repeat(row(d_skip), P_B, axis=1)
    const = lambda b, c: (0, 0)
    bc_w = CONV_DIM - D_INNER
    return pl.pallas_call(
        _ssd_kernel,
        grid=(batch, seq // L),
        in_specs=[
            pl.BlockSpec((None, L, D_INNER), lambda b, c: (b, c, 0)),
            pl.BlockSpec((None, L, D_INNER), lambda b, c: (b, c, 1)),
            pl.BlockSpec((None, L, bc_w), lambda b, c: (b, c, 2 * D_INNER // bc_w)),
            pl.BlockSpec((None, L, LANES), lambda b, c: (b, c, (D_INNER + CONV_DIM) // LANES)),
            pl.BlockSpec((CONV_B, CONV_DIM), const),
            pl.BlockSpec((1, CONV_DIM), const),
            pl.BlockSpec((1, LANES), const),
            pl.BlockSpec((1, LANES), const),
            pl.BlockSpec((1, D_INNER), const),
            pl.BlockSpec((1, D_INNER), const),
        ],
        out_specs=pl.BlockSpec((None, L, D_INNER), lambda b, c: (b, c, 0)),
        out_shape=jax.ShapeDtypeStruct((batch, seq, D_INNER), BF16),
        scratch_shapes=[pltpu.VMEM((HALO + L, CONV_DIM), F32), pltpu.VMEM((N_B, D_INNER), F32)],
        compiler_params=_cparams(("parallel", "arbitrary")),
        name="ssd",
    )(ssm, ssm, ssm, ssm, conv_w, row(conv_b), dtb, alog, dskip, row(norm_g))


def _merge_ln_kernel(h_ref, oa_ref, ob_ref, oc_ref, wg_ref, gb_ref, wbr_ref, wout_ref, g_ref, b_ref, o_ref):
    h = h_ref[...]
    hb = h.astype(BF16)
    merged = jnp.zeros(h.shape, F32)
    for r, br_ref in enumerate((oa_ref, ob_ref, oc_ref)):
        gpre = _dot(hb, wg_ref[:, r * D_MODEL:(r + 1) * D_MODEL]) + gb_ref[r:r + 1, :]
        merged = merged + jax.nn.sigmoid(gpre) * _dot(br_ref[...], wbr_ref[r])
    mix = _dot(merged.astype(BF16), wout_ref[...])
    o_ref[...] = _layernorm(DEEPNORM_ALPHA * h + mix, g_ref[...], b_ref[...])


def _merge_ln(h, oa, ob, oc, w_gate, gate_bias, w_branch, w_out, g, b):
    t = h.shape[0]
    tm = min(MERGE_TM, t)
    rowspec = pl.BlockSpec((tm, D_MODEL), lambda i: (i, 0))
    const2 = lambda i: (0, 0)
    one = pl.Buffered(1)
    return pl.pallas_call(
        _merge_ln_kernel,
        grid=(t // tm,),
        in_specs=[
            rowspec, rowspec, rowspec, rowspec,
            pl.BlockSpec((D_MODEL, N_BRANCH * D_MODEL), const2, pipeline_mode=one),
            pl.BlockSpec((N_BRANCH, D_MODEL), const2),
            pl.BlockSpec((N_BRANCH, D_MODEL, D_MODEL), lambda i: (0, 0, 0), pipeline_mode=one),
            pl.BlockSpec((D_MODEL, D_MODEL), const2, pipeline_mode=one),
            pl.BlockSpec((1, D_MODEL), const2),
            pl.BlockSpec((1, D_MODEL), const2),
        ],
        out_specs=rowspec,
        out_shape=jax.ShapeDtypeStruct((t, D_MODEL), F32),
        compiler_params=_cparams(("parallel",)),
        name="merge_ln",
    )(h, oa, ob, oc, w_gate, gate_bias, w_branch, w_out, g.reshape(1, D_MODEL), b.reshape(1, D_MODEL))


def _mixer_weights(w_in):
    segs = (A_W, A_W, A_W, D_INNER, CONV_DIM, H_B, C_W, C_W, C_W, N_BRANCH * D_MODEL)
    offs = [0]
    for s in segs:
        offs.append(offs[-1] + s)
    qa, ka, va, zb, xbc, dtb, qc, kc, vc, gate = (w_in[:, offs[n]:offs[n + 1]] for n in range(len(segs)))
    w_qk = jnp.concatenate([qa, ka], axis=1).astype(BF16)
    w_att = jnp.concatenate([va, qc, kc, vc], axis=1).astype(BF16)
    w_ssm = jnp.concatenate([zb, xbc, jnp.pad(dtb, ((0, 0), (0, LANES - H_B)))], axis=1).astype(BF16)
    return w_qk, w_att, w_ssm, gate.astype(BF16)


def kernel(x, ffn1_w_in, ffn1_w_out, ln1_g, ln1_b, w_mix_in, gate_bias, diff_lambda, diff_subln_g,
           ssm_conv_w, ssm_conv_b, ssm_dt_bias, ssm_A_log, ssm_D, ssm_norm_g, w_branch, w_mix_out,
           ln2_g, ln2_b, ffn2_w_in, ffn2_w_out, ln3_g, ln3_b):
    batch, seq, _ = x.shape
    t = batch * seq
    tabs = _rope_tables(seq)
    xf = x.reshape(t, D_MODEL)
    for l in range(DEPTH):
        xf = _ffn_ln(xf, ffn1_w_in[l].astype(BF16), ffn1_w_out[l].astype(BF16), ln1_g[l], ln1_b[l])

        w_qk, w_att, w_ssm, w_gate = _mixer_weights(w_mix_in[l])
        qk, att = _proj_attn(xf, w_qk, w_att, tabs, seq)
        qk = qk.reshape(batch, seq, 2 * A_W)
        att = att.reshape(batch, seq, A_W + 3 * C_W)
        ssm = _proj_ssm(xf, w_ssm).reshape(batch, seq, w_ssm.shape[1])

        o_a = _diff_attn(qk, att, 0, diff_lambda[l], diff_subln_g[l], l, batch, seq)
        nb = A_W // LANES
        o_c = _stick_breaking(att, nb, 2 * nb, 3 * nb, batch, seq)
        o_b = _ssd(ssm, ssm_conv_w[l], ssm_conv_b[l], ssm_dt_bias[l], ssm_A_log[l], ssm_D[l],
                   ssm_norm_g[l], batch, seq)

        xf = _merge_ln(xf, o_a.reshape(t, A_W), o_b.reshape(t, D_INNER), o_c.reshape(t, C_W),
                       w_gate, gate_bias[l], w_branch[l].astype(BF16), w_mix_out[l].astype(BF16),
                       ln2_g[l], ln2_b[l])

        xf = _ffn_ln(xf, ffn2_w_in[l].astype(BF16), ffn2_w_out[l].astype(BF16), ln3_g[l], ln3_b[l])
    return xf.reshape(batch, seq, D_MODEL)
```

```python
import functools
import math

import jax
import jax.numpy as jnp
from jax import lax
from jax.experimental import pallas as pl
from jax.experimental.pallas import tpu as pltpu

F32 = jnp.float32
BF16 = jnp.bfloat16

D_MODEL = 1024
DEPTH = 2
LN_EPS = 1e-5
DEEPNORM_ALPHA = (2 * DEPTH) ** 0.25
D_FF = 2816
FFN_RES = 0.5

CHUNK = 64
H_A = 8
HD_A = 64
ROT_A = HD_A // 4
ROPE_THETA = 500000.0
A_W = H_A * 2 * HD_A

H_B = 16
P_B = 64
D_INNER = H_B * P_B
G_B = 2
N_B = 128
CONV_B = 4
CONV_DIM = D_INNER + 2 * G_B * N_B

H_C = 16
HD_C = 64
C_W = H_C * HD_C
N_BRANCH = 3

LANES = 128
VMEM_LIMIT = 56 * 1024 * 1024

FFN_TM = 512
FFN_CHUNK = 256
PROJ_TM = 512
PROJ_SSM_CHUNKS = 3
MERGE_TM = 512
ATT_A_T = 512
ATT_A_NT = 4
ATT_C_T = 512
ATT_C_TK = 256
ATT_C_NT = 4
SSD_L = 256

assert D_FF % FFN_CHUNK == 0
assert ATT_C_T % ATT_C_TK == 0 and ATT_C_NT % (ATT_C_T // ATT_C_TK) == 0
assert HD_A == HD_C

NEG_BIG = -1e30
RUN_DEAD = -160.0
LOG2E = 1.4426950408889634
Q_SCALE = HD_A ** -0.5 * LOG2E


def _cparams(sem):
    return pltpu.CompilerParams(dimension_semantics=sem, vmem_limit_bytes=VMEM_LIMIT)


def _dot(a, b):
    return jnp.dot(a, b, preferred_element_type=F32)


def _dot_nt(a, b):
    return lax.dot_general(a, b, (((1,), (1,)), ((), ())), preferred_element_type=F32)


def _split3(x):
    hi = x.astype(BF16)
    r = x - hi.astype(F32)
    mid = r.astype(BF16)
    lo = (r - mid.astype(F32)).astype(BF16)
    return hi, mid, lo


def _dot_exact_rhs(a_bf16, x):
    hi, mid, lo = _split3(x)
    return _dot(a_bf16, hi) + (_dot(a_bf16, mid) + _dot(a_bf16, lo))


def _dot_exact_lhs(x, a_bf16):
    hi, mid, lo = _split3(x)
    return _dot(hi, a_bf16) + (_dot(mid, a_bf16) + _dot(lo, a_bf16))


def _layernorm(y, g, b):
    mu = jnp.mean(y, axis=-1, keepdims=True)
    yc = y - mu
    var = jnp.mean(yc * yc, axis=-1, keepdims=True)
    return yc * lax.rsqrt(var + LN_EPS) * g + b


def _softplus(x):
    return jnp.maximum(x, 0.0) + jnp.log(1.0 + jnp.exp(-jnp.abs(x)))


def _ffn_ln_kernel(x_ref, win_ref, wout_ref, g_ref, b_ref, o_ref):
    x = x_ref[...]
    xb = x.astype(BF16)
    acc = jnp.zeros(x.shape, F32)
    for c in range(D_FF // FFN_CHUNK):
        lo = c * FFN_CHUNK
        gate = _dot(xb, win_ref[:, lo:lo + FFN_CHUNK])
        up = _dot(xb, win_ref[:, D_FF + lo:D_FF + lo + FFN_CHUNK])
        a = (gate * jax.nn.sigmoid(gate) * up).astype(BF16)
        acc = acc + _dot(a, wout_ref[lo:lo + FFN_CHUNK, :])
    y = DEEPNORM_ALPHA * x + FFN_RES * acc
    o_ref[...] = _layernorm(y, g_ref[...], b_ref[...])


def _ffn_ln(x, w_in, w_out, g, b):
    t = x.shape[0]
    tm = min(FFN_TM, t)
    const = lambda i: (0, 0)
    return pl.pallas_call(
        _ffn_ln_kernel,
        grid=(t // tm,),
        in_specs=[
            pl.BlockSpec((tm, D_MODEL), lambda i: (i, 0)),
            pl.BlockSpec((D_MODEL, 2 * D_FF), const, pipeline_mode=pl.Buffered(1)),
            pl.BlockSpec((D_FF, D_MODEL), const, pipeline_mode=pl.Buffered(1)),
            pl.BlockSpec((1, D_MODEL), const),
            pl.BlockSpec((1, D_MODEL), const),
        ],
        out_specs=pl.BlockSpec((tm, D_MODEL), lambda i: (i, 0)),
        out_shape=jax.ShapeDtypeStruct((t, D_MODEL), F32),
        compiler_params=_cparams(("parallel",)),
        name="ffn_ln",
    )(x, w_in, w_out, g.reshape(1, D_MODEL), b.reshape(1, D_MODEL))


def _proj_ssm_kernel(x_ref, w_ref, o_ref):
    xb = x_ref[...].astype(BF16)
    n = w_ref.shape[1]
    step = n // PROJ_SSM_CHUNKS
    for lo in range(0, n, step):
        o_ref[:, lo:lo + step] = _dot(xb, w_ref[:, lo:lo + step])


def _proj_ssm(x, w):
    t = x.shape[0]
    n = w.shape[1]
    tm = min(PROJ_TM, t)
    return pl.pallas_call(
        _proj_ssm_kernel,
        grid=(t // tm,),
        in_specs=[
            pl.BlockSpec((tm, D_MODEL), lambda i: (i, 0)),
            pl.BlockSpec((D_MODEL, n), lambda i: (0, 0), pipeline_mode=pl.Buffered(1)),
        ],
        out_specs=pl.BlockSpec((tm, n), lambda i: (i, 0)),
        out_shape=jax.ShapeDtypeStruct((t, n), F32),
        compiler_params=_cparams(("parallel",)),
        name="proj_ssm",
    )(x, w)


def _proj_attn_kernel(x_ref, wqk_ref, watt_ref, c_ref, s1_ref, s2_ref, qk_ref, att_ref):
    xb = x_ref[...].astype(BF16)
    for blk in range(2):
        acc = _dot(xb, wqk_ref[:, blk * A_W:(blk + 1) * A_W])
        scale = Q_SCALE if blk == 0 else 1.0
        c = c_ref[...] * scale
        s1 = s1_ref[...] * scale
        s2 = s2_ref[...] * scale
        for h in range(A_W // LANES):
            a = acc[:, h * LANES:(h + 1) * LANES]
            r = a * c + pltpu.roll(a, ROT_A // 2, axis=1) * s1 + pltpu.roll(a, LANES - ROT_A // 2, axis=1) * s2
            qk_ref[:, blk * A_W + h * LANES:blk * A_W + (h + 1) * LANES] = r.astype(qk_ref.dtype)
    for blk in range(watt_ref.shape[1] // D_MODEL):
        acc = _dot(xb, watt_ref[:, blk * D_MODEL:(blk + 1) * D_MODEL])
        if blk == A_W // D_MODEL:
            acc = acc * Q_SCALE
        att_ref[:, blk * D_MODEL:(blk + 1) * D_MODEL] = acc.astype(att_ref.dtype)


def _proj_attn(x, w_qk, w_att, tabs, seq):
    t = x.shape[0]
    tm = min(PROJ_TM, seq)
    nrow = seq // tm
    tab_spec = pl.BlockSpec((tm, LANES), lambda i: (i % nrow, 0))
    const = lambda i: (0, 0)
    nqk, natt = w_qk.shape[1], w_att.shape[1]
    return pl.pallas_call(
        _proj_attn_kernel,
        grid=(t // tm,),
        in_specs=[
            pl.BlockSpec((tm, D_MODEL), lambda i: (i, 0)),
            pl.BlockSpec((D_MODEL, nqk), const, pipeline_mode=pl.Buffered(1)),
            pl.BlockSpec((D_MODEL, natt), const, pipeline_mode=pl.Buffered(1)),
            tab_spec, tab_spec, tab_spec,
        ],
        out_specs=[pl.BlockSpec((tm, nqk), lambda i: (i, 0)), pl.BlockSpec((tm, natt), lambda i: (i, 0))],
        out_shape=[jax.ShapeDtypeStruct((t, nqk), BF16), jax.ShapeDtypeStruct((t, natt), BF16)],
        compiler_params=_cparams(("parallel",)),
        name="proj_attn",
    )(x, w_qk, w_att, *tabs)


def _rope_tables(seq):
    half = ROT_A // 2
    inv_freq = ROPE_THETA ** (-jnp.arange(half, dtype=F32) * 2.0 / ROT_A)
    ang = jnp.arange(seq, dtype=F32)[:, None] * inv_freq[None, :]
    cos, sin = jnp.cos(ang), jnp.sin(ang)
    ones = jnp.ones((seq, HD_A - ROT_A), F32)
    zeros = jnp.zeros((seq, HD_A - ROT_A), F32)
    zh = jnp.zeros((seq, half), F32)
    c = jnp.concatenate([cos, cos, ones], axis=1)
    s1 = jnp.concatenate([zh, sin, zeros], axis=1)
    s2 = jnp.concatenate([-sin, zh, zeros], axis=1)
    rep = LANES // HD_A
    return tuple(jnp.tile(v, (1, rep)) for v in (c, s1, s2))


def _diff_attn_kernel(q_ref, k_ref, v_ref, lam_ref, g_ref, o_ref, qq_ref, m_ref, l_ref, acc_ref, *, lam_init):
    t = q_ref.shape[0]
    i = pl.program_id(2)
    lane = lax.broadcasted_iota(jnp.int32, (t, LANES), 1)
    q = q_ref[...]
    zero = jnp.zeros_like(q)
    qq_ref[...] = jnp.concatenate([jnp.where(lane < HD_A, q, zero), jnp.where(lane < HD_A, zero, q)], axis=0)
    m_ref[...] = jnp.full(m_ref.shape, NEG_BIG, F32)
    l_ref[...] = jnp.zeros(l_ref.shape, F32)
    acc_ref[...] = jnp.zeros(acc_ref.shape, F32)

    def step(j, n, masked):
        start = pl.multiple_of(j * t, t)
        kb = k_ref[pl.ds(start, n * t), :]
        vb = v_ref[pl.ds(start, n * t), :]
        nrows = t if n == ATT_A_NT else 2 * t
        for r0 in range(0, 2 * t, nrows):
            rows = pl.ds(r0, nrows)
            m = m_ref[rows, :]
            s = _dot_nt(qq_ref[rows, :], kb)
            if masked:
                qchunk = lax.broadcasted_iota(jnp.int32, (nrows, t), 0) % t // CHUNK
                kchunk = lax.broadcasted_iota(jnp.int32, (nrows, t), 1) // CHUNK
                s = jnp.where(kchunk <= qchunk, s, NEG_BIG)
            m_new = jnp.maximum(m, jnp.max(s, axis=1, keepdims=True))
            p = jnp.concatenate([jnp.exp2(s[:, c:c + LANES] - m_new) for c in range(0, n * t, LANES)], axis=1)
            corr = jnp.exp2(m - m_new)
            l_ref[rows, :] = corr * l_ref[rows, :] + jnp.sum(p, axis=1, keepdims=True)
            acc_ref[rows, :] = corr * acc_ref[rows, :] + _dot(p.astype(BF16), vb)
            m_ref[rows, :] = m_new

    done = 0
    n = ATT_A_NT
    while n >= 1:
        cnt = (i - done) // n

        def body(g, c, n=n, done=done):
            step(done + g * n, n, False)
            return c

        lax.fori_loop(0, cnt, body, 0)
        done = done + cnt * n
        n //= 2
    step(i, 1, True)
    o = acc_ref[...] / l_ref[...]
    lf = lam_ref[...]
    lam_full = (jnp.exp(jnp.sum(lf[0:1] * lf[1:2], axis=1, keepdims=True))
                - jnp.exp(jnp.sum(lf[2:3] * lf[3:4], axis=1, keepdims=True)) + lam_init)
    d = o[:t] - lam_full * o[t:]
    y = d * lax.rsqrt(jnp.mean(d * d, axis=-1, keepdims=True) + LN_EPS) * g_ref[...]
    o_ref[...] = (y * (1.0 - lam_init)).astype(o_ref.dtype)


def _diff_attn(qk, v_arr, v_col0, lam, subln_g, layer, batch, seq):
    t = min(ATT_A_T, seq)
    lam_init = 0.8 - 0.6 * math.exp(-0.3 * layer)
    kern = functools.partial(_diff_attn_kernel, lam_init=lam_init)
    return pl.pallas_call(
        kern,
        grid=(batch, H_A, seq // t),
        in_specs=[
            pl.BlockSpec((None, t, LANES), lambda b, h, i: (b, i, h)),
            pl.BlockSpec((None, seq, LANES), lambda b, h, i: (b, 0, H_A + h)),
            pl.BlockSpec((None, seq, LANES), lambda b, h, i: (b, 0, v_col0 + h)),
            pl.BlockSpec((4, HD_A), lambda b, h, i: (0, 0)),
            pl.BlockSpec((1, 2 * HD_A), lambda b, h, i: (0, 0)),
        ],
        out_specs=pl.BlockSpec((None, t, LANES), lambda b, h, i: (b, i, h)),
        out_shape=jax.ShapeDtypeStruct((batch, seq, A_W), BF16),
        scratch_shapes=[pltpu.VMEM((2 * t, LANES), BF16)] + [pltpu.VMEM((2 * t, LANES), F32)] * 3,
        compiler_params=_cparams(("parallel", "parallel", "parallel")),
        name="diff_attn",
    )(qk, qk, v_arr, lam, subln_g.reshape(1, 2 * HD_A))


def _stick_kernel(q_ref, k_ref, v_ref, o_ref, qq_ref, run_ref, acc_ref):
    tq = q_ref.shape[0]
    tk = min(ATT_C_TK, tq)
    per = tq // tk
    i = pl.program_id(2)
    lane = lax.broadcasted_iota(jnp.int32, (tq, LANES), 1)
    q = q_ref[...]
    zero = jnp.zeros_like(q)
    qq_ref[...] = jnp.concatenate([jnp.where(lane < HD_C, q, zero), jnp.where(lane < HD_C, zero, q)], axis=0)
    u = jnp.where(lax.broadcasted_iota(jnp.int32, (tk, tk), 0) >= lax.broadcasted_iota(jnp.int32, (tk, tk), 1),
                  -1.0, 0.0).astype(BF16)

    def front(qs, j, tri):
        kb = k_ref[pl.ds(pl.multiple_of(j * tk, tk), tk), :]
        z = _dot_nt(qs, kb)
        neg_abs = lax.bitcast_convert_type(
            lax.bitcast_convert_type(z, jnp.uint32) | jnp.uint32(0x80000000), F32)
        sp = jnp.maximum(z, 0.0) + jnp.log2(1.0 + jnp.exp2(neg_abs))
        if tri is not None:
            sp = jnp.where(tri, sp, 0.0)
        tail = _dot(sp.astype(BF16), u)
        return z + tail, tail[:, 0:1]

    def back(j, w, run, tri):
        a = jnp.concatenate([jnp.exp2(w[:, c:c + LANES] + run) for c in range(0, tk, LANES)], axis=1)
        if tri is not None:
            a = jnp.where(tri, a, 0.0)
        return _dot(a.astype(BF16), v_ref[pl.ds(pl.multiple_of(j * tk, tk), tk), :])

    def group(qs, js, tris, run, acc):
        fronts = [front(qs, j, tri) for j, tri in zip(js, tris)]
        for j, tri, (w, mass) in zip(js, tris, fronts):
            acc = acc + back(j, w, run, tri)
            run = run + mass
        return run, acc

    def full_group(first, n):
        run, acc = group(qq_ref[...], [first - m for m in range(n)], [None] * n, run_ref[...], acc_ref[...])
        run_ref[...] = run
        acc_ref[...] = acc

    tri = (lax.broadcasted_iota(jnp.int32, (2 * tk, tk), 1)
           < lax.broadcasted_iota(jnp.int32, (2 * tk, tk), 0) % tk)
    zeros = jnp.zeros((2 * tk, LANES), F32)
    for r in range(per):
        even, odd = pl.ds(r * tk, tk), pl.ds(tq + r * tk, tk)
        qs = jnp.concatenate([qq_ref[even, :], qq_ref[odd, :]], axis=0)
        run, acc = group(qs, [per * i + r - m for m in range(r + 1)], [tri] + [None] * r, zeros, zeros)
        run_ref[even, :], run_ref[odd, :] = run[:tk], run[tk:]
        acc_ref[even, :], acc_ref[odd, :] = acc[:tk], acc[tk:]

    def walk(first, n, count, live):
        def cond(c):
            g, live = c
            return jnp.logical_and(g < count, live > 0)

        def body(c):
            g, _ = c
            full_group(first - n * g, n)
            return g + 1, (jnp.max(run_ref[...]) > RUN_DEAD).astype(jnp.int32)

        g, live = lax.while_loop(cond, body, (jnp.int32(0), live))
        return first - n * g, live

    nxt, live = walk(per * i - 1, per, jnp.minimum(i, 1), jnp.int32(1))
    nxt, live = walk(nxt, ATT_C_NT, (nxt + 1) // ATT_C_NT, live)
    walk(nxt, per, (nxt + 1) // per, live)
    o_ref[...] = jnp.where(lane < HD_C, acc_ref[0:tq, :], acc_ref[tq:2 * tq, :]).astype(o_ref.dtype)


def _stick_breaking(arr, q_col0, k_col0, v_col0, batch, seq):
    t = min(ATT_C_T, seq)
    nhp = C_W // LANES
    return pl.pallas_call(
        _stick_kernel,
        grid=(batch, nhp, seq // t),
        in_specs=[
            pl.BlockSpec((None, t, LANES), lambda b, h, i: (b, i, q_col0 + h)),
            pl.BlockSpec((None, seq, LANES), lambda b, h, i: (b, 0, k_col0 + h)),
            pl.BlockSpec((None, seq, LANES), lambda b, h, i: (b, 0, v_col0 + h)),
        ],
        out_specs=pl.BlockSpec((None, t, LANES), lambda b, h, i: (b, i, h)),
        out_shape=jax.ShapeDtypeStruct((batch, seq, C_W), BF16),
        scratch_shapes=[pltpu.VMEM((2 * t, LANES), BF16)] + [pltpu.VMEM((2 * t, LANES), F32)] * 2,
        compiler_params=_cparams(("parallel", "parallel", "parallel")),
        name="stick_breaking",
    )(arr, arr, arr)


HALO = 8


def _ssd_kernel(z_ref, x_ref, bc_ref, dt_ref, cw_ref, cb_ref, dtb_ref, alog_ref, dskip_ref, ng_ref,
                o_ref, xe_ref, st_ref):
    L = z_ref.shape[0]
    c = pl.program_id(1)

    @pl.when(c == 0)
    def _():
        xe_ref[0:HALO, :] = jnp.zeros((HALO, CONV_DIM), F32)
        st_ref[...] = jnp.zeros(st_ref.shape, F32)

    xe_ref[HALO:HALO + L, 0:D_INNER] = x_ref[...]
    xe_ref[HALO:HALO + L, D_INNER:CONV_DIM] = bc_ref[...]
    conv = cb_ref[...]
    for r in range(CONV_B):
        off = HALO - (CONV_B - 1) + r
        conv = conv + cw_ref[r:r + 1, :] * xe_ref[off:off + L, :]
    xe_ref[0:HALO, :] = xe_ref[L:L + HALO, :]
    xc = conv * jax.nn.sigmoid(conv)
    xs = xc[:, :D_INNER]

    dt = _softplus(dt_ref[...] + dtb_ref[...])
    a = dt * (-jnp.exp(alog_ref[...]))
    row_i = lax.broadcasted_iota(jnp.int32, (L, L), 0)
    col_i = lax.broadcasted_iota(jnp.int32, (L, L), 1)
    causal = col_i <= row_i
    tri = jnp.where(causal, 1.0, 0.0).astype(BF16)
    acum = _dot_exact_rhs(tri, a)
    acum_t = acum.T
    a_last = acum[L - 1:L, :]
    e_in = jnp.exp(acum)
    e_out = jnp.exp(a_last - acum)

    expand = jnp.where(lax.broadcasted_iota(jnp.int32, (LANES, D_INNER), 0)
                       == lax.broadcasted_iota(jnp.int32, (LANES, D_INNER), 1) // P_B,
                       1.0, 0.0).astype(BF16)
    dt_x = _dot_exact_lhs(dt, expand)
    e_in_x = _dot_exact_lhs(e_in, expand)
    e_out_x = _dot_exact_lhs(e_out, expand)

    xdt = xs * dt_x
    xdt_b = xdt.astype(BF16)
    xdt_out_b = (xdt * e_out_x).astype(BF16)
    lane = lax.broadcasted_iota(jnp.int32, (L, LANES), 1)

    hg = H_B // G_B
    gw = D_INNER // G_B
    y_parts = []
    for g in range(G_B):
        bm_f = xc[:, D_INNER + g * N_B:D_INNER + (g + 1) * N_B]
        bm = bm_f.astype(BF16)
        bm_t = bm_f.T.astype(BF16)
        cm = xc[:, D_INNER + G_B * N_B + g * N_B:D_INNER + G_B * N_B + (g + 1) * N_B].astype(BF16)
        cb = _dot_nt(cm, bm)
        st_g = st_ref[:, g * gw:(g + 1) * gw]
        y_off = _dot(cm, st_g.astype(BF16)) * e_in_x[:, g * gw:(g + 1) * gw]
        y_diag = []
        for hp in range(hg // 2):
            h0 = g * hg + 2 * hp
            pair = xdt_b[:, h0 * P_B:(h0 + 2) * P_B]
            ys = []
            for h in (h0, h0 + 1):
                seg = acum[:, h:h + 1] - acum_t[h:h + 1, :]
                m = (cb * jnp.exp(jnp.where(causal, seg, -jnp.inf))).astype(BF16)
                ys.append(_dot(m, pair))
            y_diag.append(jnp.where(lane < P_B, ys[0], ys[1]))
        y_parts.append(jnp.concatenate(y_diag, axis=1) + y_off)
        upd = _dot(bm_t, xdt_out_b[:, g * gw:(g + 1) * gw])
        st_ref[:, g * gw:(g + 1) * gw] = st_g * e_in_x[L - 1:L, g * gw:(g + 1) * gw] + upd

    y = jnp.concatenate(y_parts, axis=1) + dskip_ref[...] * xs
    zz = z_ref[...]
    y = y * (zz * jax.nn.sigmoid(zz))
    outs = []
    for g in range(G_B):
        yg = y[:, g * gw:(g + 1) * gw]
        outs.append(yg * lax.rsqrt(jnp.mean(yg * yg, axis=-1, keepdims=True) + LN_EPS))
    o_ref[...] = (jnp.concatenate(outs, axis=1) * ng_ref[...]).astype(o_ref.dtype)


def _ssd(ssm, conv_w, conv_b, dt_bias, a_log, d_skip, norm_g, batch, seq):
    L = min(SSD_L, seq)
    pad = LANES - H_B
    row = lambda v: v.reshape(1, -1).astype(F32)
    dtb = jnp.pad(row(dt_bias), ((0, 0), (0, pad)))
    alog = jnp.pad(row(a_log), ((0, 0), (0, pad)))
    dskip = jnp.repeat(row(d_skip), P_B, axis=1)
    const = lambda b, c: (0, 0)
    bc_w = CONV_DIM - D_INNER
    return pl.pallas_call(
        _ssd_kernel,
        grid=(batch, seq // L),
        in_specs=[
            pl.BlockSpec((None, L, D_INNER), lambda b, c: (b, c, 0)),
            pl.BlockSpec((None, L, D_INNER), lambda b, c: (b, c, 1)),
            pl.BlockSpec((None, L, bc_w), lambda b, c: (b, c, 2 * D_INNER // bc_w)),
            pl.BlockSpec((None, L, LANES), lambda b, c: (b, c, (D_INNER + CONV_DIM) // LANES)),
            pl.BlockSpec((CONV_B, CONV_DIM), const),
            pl.BlockSpec((1, CONV_DIM), const),
            pl.BlockSpec((1, LANES), const),
            pl.BlockSpec((1, LANES), const),
            pl.BlockSpec((1, D_INNER), const),
            pl.BlockSpec((1, D_INNER), const),
        ],
        out_specs=pl.BlockSpec((None, L, D_INNER), lambda b, c: (b, c, 0)),
        out_shape=jax.ShapeDtypeStruct((batch, seq, D_INNER), BF16),
        scratch_shapes=[pltpu.VMEM((HALO + L, CONV_DIM), F32), pltpu.VMEM((N_B, D_INNER), F32)],
        compiler_params=_cparams(("parallel", "arbitrary")),
        name="ssd",
    )(ssm, ssm, ssm, ssm, conv_w, row(conv_b), dtb, alog, dskip, row(norm_g))


def _merge_ln_kernel(h_ref, oa_ref, ob_ref, oc_ref, wg_ref, gb_ref, wbr_ref, wout_ref, g_ref, b_ref, o_ref):
    h = h_ref[...]
    hb = h.astype(BF16)
    merged = jnp.zeros(h.shape, F32)
    for r, br_ref in enumerate((oa_ref, ob_ref, oc_ref)):
        gpre = _dot(hb, wg_ref[:, r * D_MODEL:(r + 1) * D_MODEL]) + gb_ref[r:r + 1, :]
        merged = merged + jax.nn.sigmoid(gpre) * _dot(br_ref[...], wbr_ref[r])
    mix = _dot(merged.astype(BF16), wout_ref[...])
    o_ref[...] = _layernorm(DEEPNORM_ALPHA * h + mix, g_ref[...], b_ref[...])


def _merge_ln(h, oa, ob, oc, w_gate, gate_bias, w_branch, w_out, g, b):
    t = h.shape[0]
    tm = min(MERGE_TM, t)
    rowspec = pl.BlockSpec((tm, D_MODEL), lambda i: (i, 0))
    const2 = lambda i: (0, 0)
    one = pl.Buffered(1)
    return pl.pallas_call(
        _merge_ln_kernel,
        grid=(t // tm,),
        in_specs=[
            rowspec, rowspec, rowspec, rowspec,
            pl.BlockSpec((D_MODEL, N_BRANCH * D_MODEL), const2, pipeline_mode=one),
            pl.BlockSpec((N_BRANCH, D_MODEL), const2),
            pl.BlockSpec((N_BRANCH, D_MODEL, D_MODEL), lambda i: (0, 0, 0), pipeline_mode=one),
            pl.BlockSpec((D_MODEL, D_MODEL), const2, pipeline_mode=one),
            pl.BlockSpec((1, D_MODEL), const2),
            pl.BlockSpec((1, D_MODEL), const2),
        ],
        out_specs=rowspec,
        out_shape=jax.ShapeDtypeStruct((t, D_MODEL), F32),
        compiler_params=_cparams(("parallel",)),
        name="merge_ln",
    )(h, oa, ob, oc, w_gate, gate_bias, w_branch, w_out, g.reshape(1, D_MODEL), b.reshape(1, D_MODEL))


def _mixer_weights(w_in):
    segs = (A_W, A_W, A_W, D_INNER, CONV_DIM, H_B, C_W, C_W, C_W, N_BRANCH * D_MODEL)
    offs = [0]
    for s in segs:
        offs.append(offs[-1] + s)
    qa, ka, va, zb, xbc, dtb, qc, kc, vc, gate = (w_in[:, offs[n]:offs[n + 1]] for n in range(len(segs)))
    w_qk = jnp.concatenate([qa, ka], axis=1).astype(BF16)
    w_att = jnp.concatenate([va, qc, kc, vc], axis=1).astype(BF16)
    w_ssm = jnp.concatenate([zb, xbc, jnp.pad(dtb, ((0, 0), (0, LANES - H_B)))], axis=1).astype(BF16)
    return w_qk, w_att, w_ssm, gate.astype(BF16)


def kernel(x, ffn1_w_in, ffn1_w_out, ln1_g, ln1_b, w_mix_in, gate_bias, diff_lambda, diff_subln_g,
           ssm_conv_w, ssm_conv_b, ssm_dt_bias, ssm_A_log, ssm_D, ssm_norm_g, w_branch, w_mix_out,
           ln2_g, ln2_b, ffn2_w_in, ffn2_w_out, ln3_g, ln3_b):
    batch, seq, _ = x.shape
    t = batch * seq
    tabs = _rope_tables(seq)
    xf = x.reshape(t, D_MODEL)
    for l in range(DEPTH):
        xf = _ffn_ln(xf, ffn1_w_in[l].astype(BF16), ffn1_w_out[l].astype(BF16), ln1_g[l], ln1_b[l])

        w_qk, w_att, w_ssm, w_gate = _mixer_weights(w_mix_in[l])
        qk, att = _proj_attn(xf, w_qk, w_att, tabs, seq)
        qk = qk.reshape(batch, seq, 2 * A_W)
        att = att.reshape(batch, seq, A_W + 3 * C_W)
        ssm = _proj_ssm(xf, w_ssm).reshape(batch, seq, w_ssm.shape[1])

        o_a = _diff_attn(qk, att, 0, diff_lambda[l], diff_subln_g[l], l, batch, seq)
        nb = A_W // LANES
        o_c = _stick_breaking(att, nb, 2 * nb, 3 * nb, batch, seq)
        o_b = _ssd(ssm, ssm_conv_w[l], ssm_conv_b[l], ssm_dt_bias[l], ssm_A_log[l], ssm_D[l],
                   ssm_norm_g[l], batch, seq)

        xf = _merge_ln(xf, o_a.reshape(t, A_W), o_b.reshape(t, D_INNER), o_c.reshape(t, C_W),
                       w_gate, gate_bias[l], w_branch[l].astype(BF16), w_mix_out[l].astype(BF16),
                       ln2_g[l], ln2_b[l])

        xf = _ffn_ln(xf, ffn2_w_in[l].astype(BF16), ffn2_w_out[l].astype(BF16), ln3_g[l], ln3_b[l])
    return xf.reshape(batch, seq, D_MODEL)
```

```python
import functools
import math

import jax
import jax.numpy as jnp
from jax import lax
from jax.experimental import pallas as pl
from jax.experimental.pallas import tpu as pltpu

F32 = jnp.float32
BF16 = jnp.bfloat16

D_MODEL = 1024
DEPTH = 2
LN_EPS = 1e-5
DEEPNORM_ALPHA = (2 * DEPTH) ** 0.25
D_FF = 2816
FFN_RES = 0.5

CHUNK = 64
H_A = 8
HD_A = 64
ROT_A = HD_A // 4
ROPE_THETA = 500000.0
A_W = H_A * 2 * HD_A

H_B = 16
P_B = 64
D_INNER = H_B * P_B
G_B = 2
N_B = 128
CONV_B = 4
CONV_DIM = D_INNER + 2 * G_B * N_B

H_C = 16
HD_C = 64
C_W = H_C * HD_C
N_BRANCH = 3

LANES = 128
VMEM_LIMIT = 56 * 1024 * 1024

FFN_TM = 512
FFN_CHUNK = 256
PROJ_TM = 512
PROJ_SSM_CHUNKS = 3
MERGE_TM = 512
ATT_A_T = 512
ATT_A_NT = 4
ATT_C_T = 512
ATT_C_TK = 256
ATT_C_NT = 4
SSD_L = 256

assert D_FF % FFN_CHUNK == 0
assert ATT_C_T % ATT_C_TK == 0 and ATT_C_NT % (ATT_C_T // ATT_C_TK) == 0
assert HD_A == HD_C

NEG_BIG = -1e30
RUN_DEAD = -160.0
LOG2E = 1.4426950408889634
Q_SCALE = HD_A ** -0.5 * LOG2E


def _cparams(sem):
    return pltpu.CompilerParams(dimension_semantics=sem, vmem_limit_bytes=VMEM_LIMIT)


def _dot(a, b):
    return jnp.dot(a, b, preferred_element_type=F32)


def _dot_nt(a, b):
    return lax.dot_general(a, b, (((1,), (1,)), ((), ())), preferred_element_type=F32)


def _split3(x):
    hi = x.astype(BF16)
    r = x - hi.astype(F32)
    mid = r.astype(BF16)
    lo = (r - mid.astype(F32)).astype(BF16)
    return hi, mid, lo


def _dot_exact_rhs(a_bf16, x):
    hi, mid, lo = _split3(x)
    return _dot(a_bf16, hi) + (_dot(a_bf16, mid) + _dot(a_bf16, lo))


def _dot_exact_lhs(x, a_bf16):
    hi, mid, lo = _split3(x)
    return _dot(hi, a_bf16) + (_dot(mid, a_bf16) + _dot(lo, a_bf16))


def _layernorm(y, g, b):
    mu = jnp.mean(y, axis=-1, keepdims=True)
    yc = y - mu
    var = jnp.mean(yc * yc, axis=-1, keepdims=True)
    return yc * lax.rsqrt(var + LN_EPS) * g + b


def _softplus(x):
    return jnp.maximum(x, 0.0) + jnp.log(1.0 + jnp.exp(-jnp.abs(x)))


def _ffn_ln_kernel(x_ref, win_ref, wout_ref, g_ref, b_ref, o_ref):
    x = x_ref[...]
    xb = x.astype(BF16)
    acc = jnp.zeros(x.shape, F32)
    for c in range(D_FF // FFN_CHUNK):
        lo = c * FFN_CHUNK
        gate = _dot(xb, win_ref[:, lo:lo + FFN_CHUNK])
        up = _dot(xb, win_ref[:, D_FF + lo:D_FF + lo + FFN_CHUNK])
        a = (gate * jax.nn.sigmoid(gate) * up).astype(BF16)
        acc = acc + _dot(a, wout_ref[lo:lo + FFN_CHUNK, :])
    y = DEEPNORM_ALPHA * x + FFN_RES * acc
    o_ref[...] = _layernorm(y, g_ref[...], b_ref[...])


def _ffn_ln(x, w_in, w_out, g, b):
    t = x.shape[0]
    tm = min(FFN_TM, t)
    const = lambda i: (0, 0)
    return pl.pallas_call(
        _ffn_ln_kernel,
        grid=(t // tm,),
        in_specs=[
            pl.BlockSpec((tm, D_MODEL), lambda i: (i, 0)),
            pl.BlockSpec((D_MODEL, 2 * D_FF), const, pipeline_mode=pl.Buffered(1)),
            pl.BlockSpec((D_FF, D_MODEL), const, pipeline_mode=pl.Buffered(1)),
            pl.BlockSpec((1, D_MODEL), const),
            pl.BlockSpec((1, D_MODEL), const),
        ],
        out_specs=pl.BlockSpec((tm, D_MODEL), lambda i: (i, 0)),
        out_shape=jax.ShapeDtypeStruct((t, D_MODEL), F32),
        compiler_params=_cparams(("parallel",)),
        name="ffn_ln",
    )(x, w_in, w_out, g.reshape(1, D_MODEL), b.reshape(1, D_MODEL))


def _proj_ssm_kernel(x_ref, w_ref, o_ref):
    xb = x_ref[...].astype(BF16)
    n = w_ref.shape[1]
    step = n // PROJ_SSM_CHUNKS
    for lo in range(0, n, step):
        o_ref[:, lo:lo + step] = _dot(xb, w_ref[:, lo:lo + step])


def _proj_ssm(x, w):
    t = x.shape[0]
    n = w.shape[1]
    tm = min(PROJ_TM, t)
    return pl.pallas_call(
        _proj_ssm_kernel,
        grid=(t // tm,),
        in_specs=[
            pl.BlockSpec((tm, D_MODEL), lambda i: (i, 0)),
            pl.BlockSpec((D_MODEL, n), lambda i: (0, 0), pipeline_mode=pl.Buffered(1)),
        ],
        out_specs=pl.BlockSpec((tm, n), lambda i: (i, 0)),
        out_shape=jax.ShapeDtypeStruct((t, n), F32),
        compiler_params=_cparams(("parallel",)),
        name="proj_ssm",
    )(x, w)


def _proj_attn_kernel(x_ref, wqk_ref, watt_ref, c_ref, s1_ref, s2_ref, qk_ref, att_ref):
    xb = x_ref[...].astype(BF16)
    for blk in range(2):
        acc = _dot(xb, wqk_ref[:, blk * A_W:(blk + 1) * A_W])
        scale = Q_SCALE if blk == 0 else 1.0
        c = c_ref[...] * scale
        s1 = s1_ref[...] * scale
        s2 = s2_ref[...] * scale
        for h in range(A_W // LANES):
            a = acc[:, h * LANES:(h + 1) * LANES]
            r = a * c + pltpu.roll(a, ROT_A // 2, axis=1) * s1 + pltpu.roll(a, LANES - ROT_A // 2, axis=1) * s2
            qk_ref[:, blk * A_W + h * LANES:blk * A_W + (h + 1) * LANES] = r.astype(qk_ref.dtype)
    for blk in range(watt_ref.shape[1] // D_MODEL):
        acc = _dot(xb, watt_ref[:, blk * D_MODEL:(blk + 1) * D_MODEL])
        if blk == A_W // D_MODEL:
            acc = acc * Q_SCALE
        att_ref[:, blk * D_MODEL:(blk + 1) * D_MODEL] = acc.astype(att_ref.dtype)


def _proj_attn(x, w_qk, w_att, tabs, seq):
    t = x.shape[0]
    tm = min(PROJ_TM, seq)
    nrow = seq // tm
    tab_spec = pl.BlockSpec((tm, LANES), lambda i: (i % nrow, 0))
    const = lambda i: (0, 0)
    nqk, natt = w_qk.shape[1], w_att.shape[1]
    return pl.pallas_call(
        _proj_attn_kernel,
        grid=(t // tm,),
        in_specs=[
            pl.BlockSpec((tm, D_MODEL), lambda i: (i, 0)),
            pl.BlockSpec((D_MODEL, nqk), const, pipeline_mode=pl.Buffered(1)),
            pl.BlockSpec((D_MODEL, natt), const, pipeline_mode=pl.Buffered(1)),
            tab_spec, tab_spec, tab_spec,
        ],
        out_specs=[pl.BlockSpec((tm, nqk), lambda i: (i, 0)), pl.BlockSpec((tm, natt), lambda i: (i, 0))],
        out_shape=[jax.ShapeDtypeStruct((t, nqk), BF16), jax.ShapeDtypeStruct((t, natt), BF16)],
        compiler_params=_cparams(("parallel",)),
        name="proj_attn",
    )(x, w_qk, w_att, *tabs)


def _rope_tables(seq):
    half = ROT_A // 2
    inv_freq = ROPE_THETA ** (-jnp.arange(half, dtype=F32) * 2.0 / ROT_A)
    ang = jnp.arange(seq, dtype=F32)[:, None] * inv_freq[None, :]
    cos, sin = jnp.cos(ang), jnp.sin(ang)
    ones = jnp.ones((seq, HD_A - ROT_A), F32)
    zeros = jnp.zeros((seq, HD_A - ROT_A), F32)
    zh = jnp.zeros((seq, half), F32)
    c = jnp.concatenate([cos, cos, ones], axis=1)
    s1 = jnp.concatenate([zh, sin, zeros], axis=1)
    s2 = jnp.concatenate([-sin, zh, zeros], axis=1)
    rep = LANES // HD_A
    return tuple(jnp.tile(v, (1, rep)) for v in (c, s1, s2))


def _diff_attn_kernel(q_ref, k_ref, v_ref, lam_ref, g_ref, o_ref, qq_ref, m_ref, l_ref, acc_ref, *, lam_init):
    t = q_ref.shape[0]
    i = pl.program_id(2)
    lane = lax.broadcasted_iota(jnp.int32, (t, LANES), 1)
    q = q_ref[...]
    zero = jnp.zeros_like(q)
    qq_ref[...] = jnp.concatenate([jnp.where(lane < HD_A, q, zero), jnp.where(lane < HD_A, zero, q)], axis=0)
    m_ref[...] = jnp.full(m_ref.shape, NEG_BIG, F32)
    l_ref[...] = jnp.zeros(l_ref.shape, F32)
    acc_ref[...] = jnp.zeros(acc_ref.shape, F32)

    def step(j, n, masked):
        start = pl.multiple_of(j * t, t)
        kb = k_ref[pl.ds(start, n * t), :]
        vb = v_ref[pl.ds(start, n * t), :]
        nrows = t if n == ATT_A_NT else 2 * t
        for r0 in range(0, 2 * t, nrows):
            rows = pl.ds(r0, nrows)
            m = m_ref[rows, :]
            s = _dot_nt(qq_ref[rows, :], kb)
            if masked:
                qchunk = lax.broadcasted_iota(jnp.int32, (nrows, t), 0) % t // CHUNK
                kchunk = lax.broadcasted_iota(jnp.int32, (nrows, t), 1) // CHUNK
                s = jnp.where(kchunk <= qchunk, s, NEG_BIG)
            m_new = jnp.maximum(m, jnp.max(s, axis=1, keepdims=True))
            p = jnp.concatenate([jnp.exp2(s[:, c:c + LANES] - m_new) for c in range(0, n * t, LANES)], axis=1)
            corr = jnp.exp2(m - m_new)
            l_ref[rows, :] = corr * l_ref[rows, :] + jnp.sum(p, axis=1, keepdims=True)
            acc_ref[rows, :] = corr * acc_ref[rows, :] + _dot(p.astype(BF16), vb)
            m_ref[rows, :] = m_new

    done = 0
    n = ATT_A_NT
    while n >= 1:
        cnt = (i - done) // n

        def body(g, c, n=n, done=done):
            step(done + g * n, n, False)
            return c

        lax.fori_loop(0, cnt, body, 0)
        done = done + cnt * n
        n //= 2
    step(i, 1, True)
    o = acc_ref[...] / l_ref[...]
    lf = lam_ref[...]
    lam_full = (jnp.exp(jnp.sum(lf[0:1] * lf[1:2], axis=1, keepdims=True))
                - jnp.exp(jnp.sum(lf[2:3] * lf[3:4], axis=1, keepdims=True)) + lam_init)
    d = o[:t] - lam_full * o[t:]
    y = d * lax.rsqrt(jnp.mean(d * d, axis=-1, keepdims=True) + LN_EPS) * g_ref[...]
    o_ref[...] = (y * (1.0 - lam_init)).astype(o_ref.dtype)


def _diff_attn(qk, v_arr, v_col0, lam, subln_g, layer, batch, seq):
    t = min(ATT_A_T, seq)
    lam_init = 0.8 - 0.6 * math.exp(-0.3 * layer)
    kern = functools.partial(_diff_attn_kernel, lam_init=lam_init)
    return pl.pallas_call(
        kern,
        grid=(batch, H_A, seq // t),
        in_specs=[
            pl.BlockSpec((None, t, LANES), lambda b, h, i: (b, i, h)),
            pl.BlockSpec((None, seq, LANES), lambda b, h, i: (b, 0, H_A + h)),
            pl.BlockSpec((None, seq, LANES), lambda b, h, i: (b, 0, v_col0 + h)),
            pl.BlockSpec((4, HD_A), lambda b, h, i: (0, 0)),
            pl.BlockSpec((1, 2 * HD_A), lambda b, h, i: (0, 0)),
        ],
        out_specs=pl.BlockSpec((None, t, LANES), lambda b, h, i: (b, i, h)),
        out_shape=jax.ShapeDtypeStruct((batch, seq, A_W), BF16),
        scratch_shapes=[pltpu.VMEM((2 * t, LANES), BF16)] + [pltpu.VMEM((2 * t, LANES), F32)] * 3,
        compiler_params=_cparams(("parallel", "parallel", "parallel")),
        name="diff_attn",
    )(qk, qk, v_arr, lam, subln_g.reshape(1, 2 * HD_A))


def _stick_kernel(q_ref, k_ref, v_ref, o_ref, qq_ref, run_ref, acc_ref):
    tq = q_ref.shape[0]
    tk = min(ATT_C_TK, tq)
    per = tq // tk
    i = pl.program_id(2)
    lane = lax.broadcasted_iota(jnp.int32, (tq, LANES), 1)
    q = q_ref[...]
    zero = jnp.zeros_like(q)
    qq_ref[...] = jnp.concatenate([jnp.where(lane < HD_C, q, zero), jnp.where(lane < HD_C, zero, q)], axis=0)
    u = jnp.where(lax.broadcasted_iota(jnp.int32, (tk, tk), 0) >= lax.broadcasted_iota(jnp.int32, (tk, tk), 1),
                  -1.0, 0.0).astype(BF16)

    def front(qs, j, tri):
        kb = k_ref[pl.ds(pl.multiple_of(j * tk, tk), tk), :]
        z = _dot_nt(qs, kb)
        neg_abs = lax.bitcast_convert_type(
            lax.bitcast_convert_type(z, jnp.uint32) | jnp.uint32(0x80000000), F32)
        sp = jnp.maximum(z, 0.0) + jnp.log2(1.0 + jnp.exp2(neg_abs))
        if tri is not None:
            sp = jnp.where(tri, sp, 0.0)
        tail = _dot(sp.astype(BF16), u)
        return z + tail, tail[:, 0:1]

    def back(j, w, run, tri):
        a = jnp.concatenate([jnp.exp2(w[:, c:c + LANES] + run) for c in range(0, tk, LANES)], axis=1)
        if tri is not None:
            a = jnp.where(tri, a, 0.0)
        return _dot(a.astype(BF16), v_ref[pl.ds(pl.multiple_of(j * tk, tk), tk), :])

    def group(qs, js, tris, run, acc, gates=None):
        fronts = [front(qs, j, tri) for j, tri in zip(js, tris)]
        for n, (j, tri, (w, mass)) in enumerate(zip(js, tris, fronts)):
            pv = back(j, w, run, tri)
            if gates is not None and gates[n] is not None:
                pv, mass = gates[n] * pv, gates[n] * mass
            acc = acc + pv
            run = run + mass
        return run, acc

    def full_group(first, n):
        run, acc = group(qq_ref[...], [first - m for m in range(n)], [None] * n, run_ref[...], acc_ref[...])
        run_ref[...] = run
        acc_ref[...] = acc

    tri = (lax.broadcasted_iota(jnp.int32, (2 * tk, tk), 1)
           < lax.broadcasted_iota(jnp.int32, (2 * tk, tk), 0) % tk)
    zeros = jnp.zeros((2 * tk, LANES), F32)
    below = jnp.maximum(per * i - 1, 0)
    has_below = (i > 0).astype(F32)
    for r in range(per):
        even, odd = pl.ds(r * tk, tk), pl.ds(tq + r * tk, tk)
        qs = jnp.concatenate([qq_ref[even, :], qq_ref[odd, :]], axis=0)
        run, acc = group(qs, [per * i + r - m for m in range(r + 1)] + [below], [tri] + [None] * (r + 1),
                         zeros, zeros, [None] * (r + 1) + [has_below])
        run_ref[even, :], run_ref[odd, :] = run[:tk], run[tk:]
        acc_ref[even, :], acc_ref[odd, :] = acc[:tk], acc[tk:]

    def alive():
        return (jnp.max(run_ref[...]) > RUN_DEAD).astype(jnp.int32)

    def walk(first, n, live):
        count = (first + 1) // n

        def cond(c):
            g, live = c
            return jnp.logical_and(g < count, live > 0)

        def body(c):
            g, _ = c
            full_group(first - n * g, n)
            return g + 1, alive()

        g, live = lax.while_loop(cond, body, (jnp.int32(0), live))
        return first - n * g, live

    nxt, live = per * i - 2, alive()
    n = ATT_C_NT
    while n >= 1:
        nxt, live = walk(nxt, n, live)
        n //= 2
    o_ref[...] = jnp.where(lane < HD_C, acc_ref[0:tq, :], acc_ref[tq:2 * tq, :]).astype(o_ref.dtype)


def _stick_breaking(arr, q_col0, k_col0, v_col0, batch, seq):
    t = min(ATT_C_T, seq)
    nhp = C_W // LANES
    return pl.pallas_call(
        _stick_kernel,
        grid=(batch, nhp, seq // t),
        in_specs=[
            pl.BlockSpec((None, t, LANES), lambda b, h, i: (b, i, q_col0 + h)),
            pl.BlockSpec((None, seq, LANES), lambda b, h, i: (b, 0, k_col0 + h)),
            pl.BlockSpec((None, seq, LANES), lambda b, h, i: (b, 0, v_col0 + h)),
        ],
        out_specs=pl.BlockSpec((None, t, LANES), lambda b, h, i: (b, i, h)),
        out_shape=jax.ShapeDtypeStruct((batch, seq, C_W), BF16),
        scratch_shapes=[pltpu.VMEM((2 * t, LANES), BF16)] + [pltpu.VMEM((2 * t, LANES), F32)] * 2,
        compiler_params=_cparams(("parallel", "parallel", "parallel")),
        name="stick_breaking",
    )(arr, arr, arr)


HALO = 8


def _ssd_kernel(z_ref, x_ref, bc_ref, dt_ref, cw_ref, cb_ref, dtb_ref, alog_ref, dskip_ref, ng_ref,
                o_ref, xe_ref, st_ref):
    L = z_ref.shape[0]
    c = pl.program_id(1)

    @pl.when(c == 0)
    def _():
        xe_ref[0:HALO, :] = jnp.zeros((HALO, CONV_DIM), F32)
        st_ref[...] = jnp.zeros(st_ref.shape, F32)

    xe_ref[HALO:HALO + L, 0:D_INNER] = x_ref[...]
    xe_ref[HALO:HALO + L, D_INNER:CONV_DIM] = bc_ref[...]
    conv = cb_ref[...]
    for r in range(CONV_B):
        off = HALO - (CONV_B - 1) + r
        conv = conv + cw_ref[r:r + 1, :] * xe_ref[off:off + L, :]
    xe_ref[0:HALO, :] = xe_ref[L:L + HALO, :]
    xc = conv * jax.nn.sigmoid(conv)
    xs = xc[:, :D_INNER]

    dt = _softplus(dt_ref[...] + dtb_ref[...])
    a = dt * (-jnp.exp(alog_ref[...]))
    row_i = lax.broadcasted_iota(jnp.int32, (L, L), 0)
    col_i = lax.broadcasted_iota(jnp.int32, (L, L), 1)
    causal = col_i <= row_i
    tri = jnp.where(causal, 1.0, 0.0).astype(BF16)
    acum = _dot_exact_rhs(tri, a)
    acum_t = acum.T
    a_last = acum[L - 1:L, :]
    e_in = jnp.exp(acum)
    e_out = jnp.exp(a_last - acum)

    expand = jnp.where(lax.broadcasted_iota(jnp.int32, (LANES, D_INNER), 0)
                       == lax.broadcasted_iota(jnp.int32, (LANES, D_INNER), 1) // P_B,
                       1.0, 0.0).astype(BF16)
    dt_x = _dot_exact_lhs(dt, expand)
    e_in_x = _dot_exact_lhs(e_in, expand)
    e_out_x = _dot_exact_lhs(e_out, expand)

    xdt = xs * dt_x
    xdt_b = xdt.astype(BF16)
    xdt_out_b = (xdt * e_out_x).astype(BF16)
    lane = lax.broadcasted_iota(jnp.int32, (L, LANES), 1)

    hg = H_B // G_B
    gw = D_INNER // G_B
    y_parts = []
    for g in range(G_B):
        bm_f = xc[:, D_INNER + g * N_B:D_INNER + (g + 1) * N_B]
        bm = bm_f.astype(BF16)
        bm_t = bm_f.T.astype(BF16)
        cm = xc[:, D_INNER + G_B * N_B + g * N_B:D_INNER + G_B * N_B + (g + 1) * N_B].astype(BF16)
        cb = _dot_nt(cm, bm)
        st_g = st_ref[:, g * gw:(g + 1) * gw]
        y_off = _dot(cm, st_g.astype(BF16)) * e_in_x[:, g * gw:(g + 1) * gw]
        y_diag = []
        for hp in range(hg // 2):
            h0 = g * hg + 2 * hp
            pair = xdt_b[:, h0 * P_B:(h0 + 2) * P_B]
            ys = []
            for h in (h0, h0 + 1):
                seg = acum[:, h:h + 1] - acum_t[h:h + 1, :]
                m = (cb * jnp.exp(jnp.where(causal, seg, -jnp.inf))).astype(BF16)
                ys.append(_dot(m, pair))
            y_diag.append(jnp.where(lane < P_B, ys[0], ys[1]))
        y_parts.append(jnp.concatenate(y_diag, axis=1) + y_off)
        upd = _dot(bm_t, xdt_out_b[:, g * gw:(g + 1) * gw])
        st_ref[:, g * gw:(g + 1) * gw] = st_g * e_in_x[L - 1:L, g * gw:(g + 1) * gw] + upd

    y = jnp.concatenate(y_parts, axis=1) + dskip_ref[...] * xs
    zz = z_ref[...]
    y = y * (zz * jax.nn.sigmoid(zz))
    outs = []
    for g in range(G_B):
        yg = y[:, g * gw:(g + 1) * gw]
        outs.append(yg * lax.rsqrt(jnp.mean(yg * yg, axis=-1, keepdims=True) + LN_EPS))
    o_ref[...] = (jnp.concatenate(outs, axis=1) * ng_ref[...]).astype(o_ref.dtype)


def _ssd(ssm, conv_w, conv_b, dt_bias, a_log, d_skip, norm_g, batch, seq):
    L = min(SSD_L, seq)
    pad = LANES - H_B
    row = lambda v: v.reshape(1, -1).astype(F32)
    dtb = jnp.pad(row(dt_bias), ((0, 0), (0, pad)))
    alog = jnp.pad(row(a_log), ((0, 0), (0, pad)))
    dskip = jnp.repeat(row(d_skip), P_B, axis=1)
    const = lambda b, c: (0, 0)
    bc_w = CONV_DIM - D_INNER
    return pl.pallas_call(
        _ssd_kernel,
        grid=(batch, seq // L),
        in_specs=[
            pl.BlockSpec((None, L, D_INNER), lambda b, c: (b, c, 0)),
            pl.BlockSpec((None, L, D_INNER), lambda b, c: (b, c, 1)),
            pl.BlockSpec((None, L, bc_w), lambda b, c: (b, c, 2 * D_INNER // bc_w)),
            pl.BlockSpec((None, L, LANES), lambda b, c: (b, c, (D_INNER + CONV_DIM) // LANES)),
            pl.BlockSpec((CONV_B, CONV_DIM), const),
            pl.BlockSpec((1, CONV_DIM), const),
            pl.BlockSpec((1, LANES), const),
            pl.BlockSpec((1, LANES), const),
            pl.BlockSpec((1, D_INNER), const),
            pl.BlockSpec((1, D_INNER), const),
        ],
        out_specs=pl.BlockSpec((None, L, D_INNER), lambda b, c: (b, c, 0)),
        out_shape=jax.ShapeDtypeStruct((batch, seq, D_INNER), BF16),
        scratch_shapes=[pltpu.VMEM((HALO + L, CONV_DIM), F32), pltpu.VMEM((N_B, D_INNER), F32)],
        compiler_params=_cparams(("parallel", "arbitrary")),
        name="ssd",
    )(ssm, ssm, ssm, ssm, conv_w, row(conv_b), dtb, alog, dskip, row(norm_g))


def _merge_ln_kernel(h_ref, oa_ref, ob_ref, oc_ref, wg_ref, gb_ref, wbr_ref, wout_ref, g_ref, b_ref, o_ref):
    h = h_ref[...]
    hb = h.astype(BF16)
    merged = jnp.zeros(h.shape, F32)
    for r, br_ref in enumerate((oa_ref, ob_ref, oc_ref)):
        gpre = _dot(hb, wg_ref[:, r * D_MODEL:(r + 1) * D_MODEL]) + gb_ref[r:r + 1, :]
        merged = merged + jax.nn.sigmoid(gpre) * _dot(br_ref[...], wbr_ref[r])
    mix = _dot(merged.astype(BF16), wout_ref[...])
    o_ref[...] = _layernorm(DEEPNORM_ALPHA * h + mix, g_ref[...], b_ref[...])


def _merge_ln(h, oa, ob, oc, w_gate, gate_bias, w_branch, w_out, g, b):
    t = h.shape[0]
    tm = min(MERGE_TM, t)
    rowspec = pl.BlockSpec((tm, D_MODEL), lambda i: (i, 0))
    const2 = lambda i: (0, 0)
    one = pl.Buffered(1)
    return pl.pallas_call(
        _merge_ln_kernel,
        grid=(t // tm,),
        in_specs=[
            rowspec, rowspec, rowspec, rowspec,
            pl.BlockSpec((D_MODEL, N_BRANCH * D_MODEL), const2, pipeline_mode=one),
            pl.BlockSpec((N_BRANCH, D_MODEL), const2),
            pl.BlockSpec((N_BRANCH, D_MODEL, D_MODEL), lambda i: (0, 0, 0), pipeline_mode=one),
            pl.BlockSpec((D_MODEL, D_MODEL), const2, pipeline_mode=one),
            pl.BlockSpec((1, D_MODEL), const2),
            pl.BlockSpec((1, D_MODEL), const2),
        ],
        out_specs=rowspec,
        out_shape=jax.ShapeDtypeStruct((t, D_MODEL), F32),
        compiler_params=_cparams(("parallel",)),
        name="merge_ln",
    )(h, oa, ob, oc, w_gate, gate_bias, w_branch, w_out, g.reshape(1, D_MODEL), b.reshape(1, D_MODEL))


def _mixer_weights(w_in):
    segs = (A_W, A_W, A_W, D_INNER, CONV_DIM, H_B, C_W, C_W, C_W, N_BRANCH * D_MODEL)
    offs = [0]
    for s in segs:
        offs.append(offs[-1] + s)
    qa, ka, va, zb, xbc, dtb, qc, kc, vc, gate = (w_in[:, offs[n]:offs[n + 1]] for n in range(len(segs)))
    w_qk = jnp.concatenate([qa, ka], axis=1).astype(BF16)
    w_att = jnp.concatenate([va, qc, kc, vc], axis=1).astype(BF16)
    w_ssm = jnp.concatenate([zb, xbc, jnp.pad(dtb, ((0, 0), (0, LANES - H_B)))], axis=1).astype(BF16)
    return w_qk, w_att, w_ssm, gate.astype(BF16)


def kernel(x, ffn1_w_in, ffn1_w_out, ln1_g, ln1_b, w_mix_in, gate_bias, diff_lambda, diff_subln_g,
           ssm_conv_w, ssm_conv_b, ssm_dt_bias, ssm_A_log, ssm_D, ssm_norm_g, w_branch, w_mix_out,
           ln2_g, ln2_b, ffn2_w_in, ffn2_w_out, ln3_g, ln3_b):
    batch, seq, _ = x.shape
    t = batch * seq
    tabs = _rope_tables(seq)
    xf = x.reshape(t, D_MODEL)
    for l in range(DEPTH):
        xf = _ffn_ln(xf, ffn1_w_in[l].astype(BF16), ffn1_w_out[l].astype(BF16), ln1_g[l], ln1_b[l])

        w_qk, w_att, w_ssm, w_gate = _mixer_weights(w_mix_in[l])
        qk, att = _proj_attn(xf, w_qk, w_att, tabs, seq)
        qk = qk.reshape(batch, seq, 2 * A_W)
        att = att.reshape(batch, seq, A_W + 3 * C_W)
        ssm = _proj_ssm(xf, w_ssm).reshape(batch, seq, w_ssm.shape[1])

        o_a = _diff_attn(qk, att, 0, diff_lambda[l], diff_subln_g[l], l, batch, seq)
        nb = A_W // LANES
        o_c = _stick_breaking(att, nb, 2 * nb, 3 * nb, batch, seq)
        o_b = _ssd(ssm, ssm_conv_w[l], ssm_conv_b[l], ssm_dt_bias[l], ssm_A_log[l], ssm_D[l],
                   ssm_norm_g[l], batch, seq)

        xf = _merge_ln(xf, o_a.reshape(t, A_W), o_b.reshape(t, D_INNER), o_c.reshape(t, C_W),
                       w_gate, gate_bias[l], w_branch[l].astype(BF16), w_mix_out[l].astype(BF16),
                       ln2_g[l], ln2_b[l])

        xf = _ffn_ln(xf, ffn2_w_in[l].astype(BF16), ffn2_w_out[l].astype(BF16), ln3_g[l], ln3_b[l])
    return xf.reshape(batch, seq, D_MODEL)
```

```python
import functools
import math

import jax
import jax.numpy as jnp
from jax import lax
from jax.experimental import pallas as pl
from jax.experimental.pallas import tpu as pltpu

F32 = jnp.float32
BF16 = jnp.bfloat16

D_MODEL = 1024
DEPTH = 2
LN_EPS = 1e-5
DEEPNORM_ALPHA = (2 * DEPTH) ** 0.25
D_FF = 2816
FFN_RES = 0.5

CHUNK = 64
H_A = 8
HD_A = 64
ROT_A = HD_A // 4
ROPE_THETA = 500000.0
A_W = H_A * 2 * HD_A

H_B = 16
P_B = 64
D_INNER = H_B * P_B
G_B = 2
N_B = 128
CONV_B = 4
CONV_DIM = D_INNER + 2 * G_B * N_B

H_C = 16
HD_C = 64
C_W = H_C * HD_C
N_BRANCH = 3

LANES = 128
VMEM_LIMIT = 56 * 1024 * 1024

FFN_TM = 512
FFN_CHUNK = 256
PROJ_TM = 512
PROJ_SSM_CHUNKS = 3
MERGE_TM = 512
ATT_A_T = 512
ATT_A_NT = 4
ATT_A_HEADS = 4
ATT_C_T = 512
ATT_C_TK = 256
ATT_C_NT = 4
SSD_L = 256

assert D_FF % FFN_CHUNK == 0
assert ATT_C_T % ATT_C_TK == 0 and ATT_C_NT % (ATT_C_T // ATT_C_TK) == 0
assert HD_A == HD_C

NEG_BIG = -1e30
RUN_DEAD = -160.0
LOG2E = 1.4426950408889634
Q_SCALE = HD_A ** -0.5 * LOG2E


def _cparams(sem):
    return pltpu.CompilerParams(dimension_semantics=sem, vmem_limit_bytes=VMEM_LIMIT)


def _dot(a, b):
    return jnp.dot(a, b, preferred_element_type=F32)


def _dot_nt(a, b):
    return lax.dot_general(a, b, (((1,), (1,)), ((), ())), preferred_element_type=F32)


def _split3(x):
    hi = x.astype(BF16)
    r = x - hi.astype(F32)
    mid = r.astype(BF16)
    lo = (r - mid.astype(F32)).astype(BF16)
    return hi, mid, lo


def _dot_exact_rhs(a_bf16, x):
    hi, mid, lo = _split3(x)
    return _dot(a_bf16, hi) + (_dot(a_bf16, mid) + _dot(a_bf16, lo))


def _dot_exact_lhs(x, a_bf16):
    hi, mid, lo = _split3(x)
    return _dot(hi, a_bf16) + (_dot(mid, a_bf16) + _dot(lo, a_bf16))


def _layernorm(y, g, b):
    mu = jnp.mean(y, axis=-1, keepdims=True)
    yc = y - mu
    var = jnp.mean(yc * yc, axis=-1, keepdims=True)
    return yc * lax.rsqrt(var + LN_EPS) * g + b


def _softplus(x):
    return jnp.maximum(x, 0.0) + jnp.log(1.0 + jnp.exp(-jnp.abs(x)))


def _ffn_ln_kernel(x_ref, win_ref, wout_ref, g_ref, b_ref, o_ref):
    x = x_ref[...]
    xb = x.astype(BF16)
    acc = jnp.zeros(x.shape, F32)
    for c in range(D_FF // FFN_CHUNK):
        lo = c * FFN_CHUNK
        gate = _dot(xb, win_ref[:, lo:lo + FFN_CHUNK])
        up = _dot(xb, win_ref[:, D_FF + lo:D_FF + lo + FFN_CHUNK])
        a = (gate * jax.nn.sigmoid(gate) * up).astype(BF16)
        acc = acc + _dot(a, wout_ref[lo:lo + FFN_CHUNK, :])
    y = DEEPNORM_ALPHA * x + FFN_RES * acc
    o_ref[...] = _layernorm(y, g_ref[...], b_ref[...])


def _ffn_ln(x, w_in, w_out, g, b):
    t = x.shape[0]
    tm = min(FFN_TM, t)
    const = lambda i: (0, 0)
    return pl.pallas_call(
        _ffn_ln_kernel,
        grid=(t // tm,),
        in_specs=[
            pl.BlockSpec((tm, D_MODEL), lambda i: (i, 0)),
            pl.BlockSpec((D_MODEL, 2 * D_FF), const, pipeline_mode=pl.Buffered(1)),
            pl.BlockSpec((D_FF, D_MODEL), const, pipeline_mode=pl.Buffered(1)),
            pl.BlockSpec((1, D_MODEL), const),
            pl.BlockSpec((1, D_MODEL), const),
        ],
        out_specs=pl.BlockSpec((tm, D_MODEL), lambda i: (i, 0)),
        out_shape=jax.ShapeDtypeStruct((t, D_MODEL), F32),
        compiler_params=_cparams(("parallel",)),
        name="ffn_ln",
    )(x, w_in, w_out, g.reshape(1, D_MODEL), b.reshape(1, D_MODEL))


def _proj_ssm_kernel(x_ref, w_ref, o_ref):
    xb = x_ref[...].astype(BF16)
    n = w_ref.shape[1]
    step = n // PROJ_SSM_CHUNKS
    for lo in range(0, n, step):
        o_ref[:, lo:lo + step] = _dot(xb, w_ref[:, lo:lo + step])


def _proj_ssm(x, w):
    t = x.shape[0]
    n = w.shape[1]
    tm = min(PROJ_TM, t)
    return pl.pallas_call(
        _proj_ssm_kernel,
        grid=(t // tm,),
        in_specs=[
            pl.BlockSpec((tm, D_MODEL), lambda i: (i, 0)),
            pl.BlockSpec((D_MODEL, n), lambda i: (0, 0), pipeline_mode=pl.Buffered(1)),
        ],
        out_specs=pl.BlockSpec((tm, n), lambda i: (i, 0)),
        out_shape=jax.ShapeDtypeStruct((t, n), F32),
        compiler_params=_cparams(("parallel",)),
        name="proj_ssm",
    )(x, w)


def _proj_attn_kernel(x_ref, wqk_ref, watt_ref, c_ref, s1_ref, s2_ref, qk_ref, att_ref):
    xb = x_ref[...].astype(BF16)
    for blk in range(2):
        acc = _dot(xb, wqk_ref[:, blk * A_W:(blk + 1) * A_W])
        scale = Q_SCALE if blk == 0 else 1.0
        c = c_ref[...] * scale
        s1 = s1_ref[...] * scale
        s2 = s2_ref[...] * scale
        for h in range(A_W // LANES):
            a = acc[:, h * LANES:(h + 1) * LANES]
            r = a * c + pltpu.roll(a, ROT_A // 2, axis=1) * s1 + pltpu.roll(a, LANES - ROT_A // 2, axis=1) * s2
            qk_ref[:, blk * A_W + h * LANES:blk * A_W + (h + 1) * LANES] = r.astype(qk_ref.dtype)
    for blk in range(watt_ref.shape[1] // D_MODEL):
        acc = _dot(xb, watt_ref[:, blk * D_MODEL:(blk + 1) * D_MODEL])
        if blk == A_W // D_MODEL:
            acc = acc * Q_SCALE
        att_ref[:, blk * D_MODEL:(blk + 1) * D_MODEL] = acc.astype(att_ref.dtype)


def _proj_attn(x, w_qk, w_att, tabs, seq):
    t = x.shape[0]
    tm = min(PROJ_TM, seq)
    nrow = seq // tm
    tab_spec = pl.BlockSpec((tm, LANES), lambda i: (i % nrow, 0))
    const = lambda i: (0, 0)
    nqk, natt = w_qk.shape[1], w_att.shape[1]
    return pl.pallas_call(
        _proj_attn_kernel,
        grid=(t // tm,),
        in_specs=[
            pl.BlockSpec((tm, D_MODEL), lambda i: (i, 0)),
            pl.BlockSpec((D_MODEL, nqk), const, pipeline_mode=pl.Buffered(1)),
            pl.BlockSpec((D_MODEL, natt), const, pipeline_mode=pl.Buffered(1)),
            tab_spec, tab_spec, tab_spec,
        ],
        out_specs=[pl.BlockSpec((tm, nqk), lambda i: (i, 0)), pl.BlockSpec((tm, natt), lambda i: (i, 0))],
        out_shape=[jax.ShapeDtypeStruct((t, nqk), BF16), jax.ShapeDtypeStruct((t, natt), BF16)],
        compiler_params=_cparams(("parallel",)),
        name="proj_attn",
    )(x, w_qk, w_att, *tabs)


def _rope_tables(seq):
    half = ROT_A // 2
    inv_freq = ROPE_THETA ** (-jnp.arange(half, dtype=F32) * 2.0 / ROT_A)
    ang = jnp.arange(seq, dtype=F32)[:, None] * inv_freq[None, :]
    cos, sin = jnp.cos(ang), jnp.sin(ang)
    ones = jnp.ones((seq, HD_A - ROT_A), F32)
    zeros = jnp.zeros((seq, HD_A - ROT_A), F32)
    zh = jnp.zeros((seq, half), F32)
    c = jnp.concatenate([cos, cos, ones], axis=1)
    s1 = jnp.concatenate([zh, sin, zeros], axis=1)
    s2 = jnp.concatenate([-sin, zh, zeros], axis=1)
    rep = LANES // HD_A
    return tuple(jnp.tile(v, (1, rep)) for v in (c, s1, s2))


def _diff_attn_kernel(q_ref, k_ref, v_ref, lam_ref, g_ref, o_ref, qq_ref, m_ref, l_ref, acc_ref, *, lam_init):
    t = q_ref.shape[0]
    nh = q_ref.shape[1] // LANES
    i = pl.program_id(2)
    lane = lax.broadcasted_iota(jnp.int32, (t, LANES), 1)
    for hh in range(nh):
        q = q_ref[:, hh * LANES:(hh + 1) * LANES]
        zero = jnp.zeros_like(q)
        qq_ref[hh] = jnp.concatenate([jnp.where(lane < HD_A, q, zero), jnp.where(lane < HD_A, zero, q)], axis=0)
    m_ref[...] = jnp.full(m_ref.shape, NEG_BIG, F32)
    l_ref[...] = jnp.zeros(l_ref.shape, F32)
    acc_ref[...] = jnp.zeros(acc_ref.shape, F32)

    def step(j, n, masked):
        start = pl.multiple_of(j * t, t)
        for hh in range(nh):
            kb = k_ref[pl.ds(start, n * t), hh * LANES:(hh + 1) * LANES]
            vb = v_ref[pl.ds(start, n * t), hh * LANES:(hh + 1) * LANES]
            m = m_ref[hh]
            s = _dot_nt(qq_ref[hh], kb)
            if masked:
                qchunk = lax.broadcasted_iota(jnp.int32, (2 * t, t), 0) % t // CHUNK
                kchunk = lax.broadcasted_iota(jnp.int32, (2 * t, t), 1) // CHUNK
                s = jnp.where(kchunk <= qchunk, s, NEG_BIG)
            m_new = jnp.maximum(m, jnp.max(s, axis=1, keepdims=True))
            p = jnp.concatenate([jnp.exp2(s[:, c:c + LANES] - m_new) for c in range(0, n * t, LANES)], axis=1)
            corr = jnp.exp2(m - m_new)
            l_ref[hh] = corr * l_ref[hh] + jnp.sum(p, axis=1, keepdims=True)
            acc_ref[hh] = corr * acc_ref[hh] + _dot(p.astype(BF16), vb)
            m_ref[hh] = m_new

    done = 0
    n = ATT_A_NT
    while n >= 1:
        cnt = (i - done) // n

        def body(g, c, n=n, done=done):
            step(done + g * n, n, False)
            return c

        lax.fori_loop(0, cnt, body, 0)
        done = done + cnt * n
        n //= 2
    step(i, 1, True)
    lf = lam_ref[...]
    lam_full = (jnp.exp(jnp.sum(lf[0:1] * lf[1:2], axis=1, keepdims=True))
                - jnp.exp(jnp.sum(lf[2:3] * lf[3:4], axis=1, keepdims=True)) + lam_init)
    for hh in range(nh):
        o = acc_ref[hh] / l_ref[hh]
        d = o[:t] - lam_full * o[t:]
        y = d * lax.rsqrt(jnp.mean(d * d, axis=-1, keepdims=True) + LN_EPS) * g_ref[...]
        o_ref[:, hh * LANES:(hh + 1) * LANES] = (y * (1.0 - lam_init)).astype(o_ref.dtype)


def _diff_attn(qk, v_arr, v_col0, lam, subln_g, layer, batch, seq):
    t = min(ATT_A_T, seq)
    nh = ATT_A_HEADS
    w = nh * LANES
    assert H_A % nh == 0 and v_col0 % nh == 0
    lam_init = 0.8 - 0.6 * math.exp(-0.3 * layer)
    kern = functools.partial(_diff_attn_kernel, lam_init=lam_init)
    return pl.pallas_call(
        kern,
        grid=(batch, H_A // nh, seq // t),
        in_specs=[
            pl.BlockSpec((None, t, w), lambda b, h, i: (b, i, h)),
            pl.BlockSpec((None, seq, w), lambda b, h, i: (b, 0, H_A // nh + h)),
            pl.BlockSpec((None, seq, w), lambda b, h, i: (b, 0, v_col0 // nh + h)),
            pl.BlockSpec((4, HD_A), lambda b, h, i: (0, 0)),
            pl.BlockSpec((1, 2 * HD_A), lambda b, h, i: (0, 0)),
        ],
        out_specs=pl.BlockSpec((None, t, w), lambda b, h, i: (b, i, h)),
        out_shape=jax.ShapeDtypeStruct((batch, seq, A_W), BF16),
        scratch_shapes=[pltpu.VMEM((nh, 2 * t, LANES), BF16)] + [pltpu.VMEM((nh, 2 * t, LANES), F32)] * 3,
        compiler_params=_cparams(("parallel", "parallel", "parallel")),
        name="diff_attn",
    )(qk, qk, v_arr, lam, subln_g.reshape(1, 2 * HD_A))


def _stick_kernel(q_ref, k_ref, v_ref, o_ref, qq_ref, run_ref, acc_ref):
    tq = q_ref.shape[0]
    tk = min(ATT_C_TK, tq)
    per = tq // tk
    i = pl.program_id(2)
    lane = lax.broadcasted_iota(jnp.int32, (tq, LANES), 1)
    q = q_ref[...]
    zero = jnp.zeros_like(q)
    qq_ref[...] = jnp.concatenate([jnp.where(lane < HD_C, q, zero), jnp.where(lane < HD_C, zero, q)], axis=0)
    u = jnp.where(lax.broadcasted_iota(jnp.int32, (tk, tk), 0) >= lax.broadcasted_iota(jnp.int32, (tk, tk), 1),
                  -1.0, 0.0).astype(BF16)

    def front(qs, j, tri):
        kb = k_ref[pl.ds(pl.multiple_of(j * tk, tk), tk), :]
        z = _dot_nt(qs, kb)
        neg_abs = lax.bitcast_convert_type(
            lax.bitcast_convert_type(z, jnp.uint32) | jnp.uint32(0x80000000), F32)
        sp = jnp.maximum(z, 0.0) + jnp.log2(1.0 + jnp.exp2(neg_abs))
        if tri is not None:
            sp = jnp.where(tri, sp, 0.0)
        tail = _dot(sp.astype(BF16), u)
        return z + tail, tail[:, 0:1]

    def back(j, w, run, tri):
        a = jnp.concatenate([jnp.exp2(w[:, c:c + LANES] + run) for c in range(0, tk, LANES)], axis=1)
        if tri is not None:
            a = jnp.where(tri, a, 0.0)
        return _dot(a.astype(BF16), v_ref[pl.ds(pl.multiple_of(j * tk, tk), tk), :])

    def group(qs, js, tris, run, acc, gates=None):
        fronts = [front(qs, j, tri) for j, tri in zip(js, tris)]
        for n, (j, tri, (w, mass)) in enumerate(zip(js, tris, fronts)):
            pv = back(j, w, run, tri)
            if gates is not None and gates[n] is not None:
                pv, mass = gates[n] * pv, gates[n] * mass
            acc = acc + pv
            run = run + mass
        return run, acc

    def full_group(first, n):
        run, acc = group(qq_ref[...], [first - m for m in range(n)], [None] * n, run_ref[...], acc_ref[...])
        run_ref[...] = run
        acc_ref[...] = acc

    tri = (lax.broadcasted_iota(jnp.int32, (2 * tk, tk), 1)
           < lax.broadcasted_iota(jnp.int32, (2 * tk, tk), 0) % tk)
    zeros = jnp.zeros((2 * tk, LANES), F32)
    below = jnp.maximum(per * i - 1, 0)
    has_below = (i > 0).astype(F32)
    for r in range(per):
        even, odd = pl.ds(r * tk, tk), pl.ds(tq + r * tk, tk)
        qs = jnp.concatenate([qq_ref[even, :], qq_ref[odd, :]], axis=0)
        run, acc = group(qs, [per * i + r - m for m in range(r + 1)] + [below], [tri] + [None] * (r + 1),
                         zeros, zeros, [None] * (r + 1) + [has_below])
        run_ref[even, :], run_ref[odd, :] = run[:tk], run[tk:]
        acc_ref[even, :], acc_ref[odd, :] = acc[:tk], acc[tk:]

    def alive():
        return (jnp.max(run_ref[...]) > RUN_DEAD).astype(jnp.int32)

    def walk(first, n, live):
        count = (first + 1) // n

        def cond(c):
            g, live = c
            return jnp.logical_and(g < count, live > 0)

        def body(c):
            g, _ = c
            full_group(first - n * g, n)
            return g + 1, alive()

        g, live = lax.while_loop(cond, body, (jnp.int32(0), live))
        return first - n * g, live

    nxt, live = per * i - 2, alive()
    n = ATT_C_NT
    while n >= 1:
        nxt, live = walk(nxt, n, live)
        n //= 2
    o_ref[...] = jnp.where(lane < HD_C, acc_ref[0:tq, :], acc_ref[tq:2 * tq, :]).astype(o_ref.dtype)


def _stick_breaking(arr, q_col0, k_col0, v_col0, batch, seq):
    t = min(ATT_C_T, seq)
    nhp = C_W // LANES
    return pl.pallas_call(
        _stick_kernel,
        grid=(batch, nhp, seq // t),
        in_specs=[
            pl.BlockSpec((None, t, LANES), lambda b, h, i: (b, i, q_col0 + h)),
            pl.BlockSpec((None, seq, LANES), lambda b, h, i: (b, 0, k_col0 + h)),
            pl.BlockSpec((None, seq, LANES), lambda b, h, i: (b, 0, v_col0 + h)),
        ],
        out_specs=pl.BlockSpec((None, t, LANES), lambda b, h, i: (b, i, h)),
        out_shape=jax.ShapeDtypeStruct((batch, seq, C_W), BF16),
        scratch_shapes=[pltpu.VMEM((2 * t, LANES), BF16)] + [pltpu.VMEM((2 * t, LANES), F32)] * 2,
        compiler_params=_cparams(("parallel", "parallel", "parallel")),
        name="stick_breaking",
    )(arr, arr, arr)


HALO = 8


def _ssd_kernel(z_ref, x_ref, bc_ref, dt_ref, cw_ref, cb_ref, dtb_ref, alog_ref, dskip_ref, ng_ref,
                o_ref, xe_ref, st_ref):
    L = z_ref.shape[0]
    c = pl.program_id(1)

    @pl.when(c == 0)
    def _():
        xe_ref[0:HALO, :] = jnp.zeros((HALO, CONV_DIM), F32)
        st_ref[...] = jnp.zeros(st_ref.shape, F32)

    xe_ref[HALO:HALO + L, 0:D_INNER] = x_ref[...]
    xe_ref[HALO:HALO + L, D_INNER:CONV_DIM] = bc_ref[...]
    conv = cb_ref[...]
    for r in range(CONV_B):
        off = HALO - (CONV_B - 1) + r
        conv = conv + cw_ref[r:r + 1, :] * xe_ref[off:off + L, :]
    xe_ref[0:HALO, :] = xe_ref[L:L + HALO, :]
    xc = conv * jax.nn.sigmoid(conv)
    xs = xc[:, :D_INNER]

    dt = _softplus(dt_ref[...] + dtb_ref[...])
    a = dt * (-jnp.exp(alog_ref[...]))
    row_i = lax.broadcasted_iota(jnp.int32, (L, L), 0)
    col_i = lax.broadcasted_iota(jnp.int32, (L, L), 1)
    causal = col_i <= row_i
    tri = jnp.where(causal, 1.0, 0.0).astype(BF16)
    acum = _dot_exact_rhs(tri, a)
    acum_t = acum.T
    a_last = acum[L - 1:L, :]
    e_in = jnp.exp(acum)
    e_out = jnp.exp(a_last - acum)

    expand = jnp.where(lax.broadcasted_iota(jnp.int32, (LANES, D_INNER), 0)
                       == lax.broadcasted_iota(jnp.int32, (LANES, D_INNER), 1) // P_B,
                       1.0, 0.0).astype(BF16)
    dt_x = _dot_exact_lhs(dt, expand)
    e_in_x = _dot_exact_lhs(e_in, expand)
    e_out_x = _dot_exact_lhs(e_out, expand)

    xdt = xs * dt_x
    xdt_b = xdt.astype(BF16)
    xdt_out_b = (xdt * e_out_x).astype(BF16)
    lane = lax.broadcasted_iota(jnp.int32, (L, LANES), 1)

    hg = H_B // G_B
    gw = D_INNER // G_B
    y_parts = []
    for g in range(G_B):
        bm_f = xc[:, D_INNER + g * N_B:D_INNER + (g + 1) * N_B]
        bm = bm_f.astype(BF16)
        bm_t = bm_f.T.astype(BF16)
        cm = xc[:, D_INNER + G_B * N_B + g * N_B:D_INNER + G_B * N_B + (g + 1) * N_B].astype(BF16)
        cb = _dot_nt(cm, bm)
        st_g = st_ref[:, g * gw:(g + 1) * gw]
        y_off = _dot(cm, st_g.astype(BF16)) * e_in_x[:, g * gw:(g + 1) * gw]
        y_diag = []
        for hp in range(hg // 2):
            h0 = g * hg + 2 * hp
            pair = xdt_b[:, h0 * P_B:(h0 + 2) * P_B]
            ys = []
            for h in (h0, h0 + 1):
                seg = acum[:, h:h + 1] - acum_t[h:h + 1, :]
                m = (cb * jnp.exp(jnp.where(causal, seg, -jnp.inf))).astype(BF16)
                ys.append(_dot(m, pair))
            y_diag.append(jnp.where(lane < P_B, ys[0], ys[1]))
        y_parts.append(jnp.concatenate(y_diag, axis=1) + y_off)
        upd = _dot(bm_t, xdt_out_b[:, g * gw:(g + 1) * gw])
        st_ref[:, g * gw:(g + 1) * gw] = st_g * e_in_x[L - 1:L, g * gw:(g + 1) * gw] + upd

    y = jnp.concatenate(y_parts, axis=1) + dskip_ref[...] * xs
    zz = z_ref[...]
    y = y * (zz * jax.nn.sigmoid(zz))
    outs = []
    for g in range(G_B):
        yg = y[:, g * gw:(g + 1) * gw]
        outs.append(yg * lax.rsqrt(jnp.mean(yg * yg, axis=-1, keepdims=True) + LN_EPS))
    o_ref[...] = (jnp.concatenate(outs, axis=1) * ng_ref[...]).astype(o_ref.dtype)


def _ssd(ssm, conv_w, conv_b, dt_bias, a_log, d_skip, norm_g, batch, seq):
    L = min(SSD_L, seq)
    pad = LANES - H_B
    row = lambda v: v.reshape(1, -1).astype(F32)
    dtb = jnp.pad(row(dt_bias), ((0, 0), (0, pad)))
    alog = jnp.pad(row(a_log), ((0, 0), (0, pad)))
    dskip = jnp.repeat(row(d_skip), P_B, axis=1)
    const = lambda b, c: (0, 0)
    bc_w = CONV_DIM - D_INNER
    return pl.pallas_call(
        _ssd_kernel,
        grid=(batch, seq // L),
        in_specs=[
            pl.BlockSpec((None, L, D_INNER), lambda b, c: (b, c, 0)),
            pl.BlockSpec((None, L, D_INNER), lambda b, c: (b, c, 1)),
            pl.BlockSpec((None, L, bc_w), lambda b, c: (b, c, 2 * D_INNER // bc_w)),
            pl.BlockSpec((None, L, LANES), lambda b, c: (b, c, (D_INNER + CONV_DIM) // LANES)),
            pl.BlockSpec((CONV_B, CONV_DIM), const),
            pl.BlockSpec((1, CONV_DIM), const),
            pl.BlockSpec((1, LANES), const),
            pl.BlockSpec((1, LANES), const),
            pl.BlockSpec((1, D_INNER), const),
            pl.BlockSpec((1, D_INNER), const),
        ],
        out_specs=pl.BlockSpec((None, L, D_INNER), lambda b, c: (b, c, 0)),
        out_shape=jax.ShapeDtypeStruct((batch, seq, D_INNER), BF16),
        scratch_shapes=[pltpu.VMEM((HALO + L, CONV_DIM), F32), pltpu.VMEM((N_B, D_INNER), F32)],
        compiler_params=_cparams(("parallel", "arbitrary")),
        name="ssd",
    )(ssm, ssm, ssm, ssm, conv_w, row(conv_b), dtb, alog, dskip, row(norm_g))


def _merge_ln_kernel(h_ref, oa_ref, ob_ref, oc_ref, wg_ref, gb_ref, wbr_ref, wout_ref, g_ref, b_ref, o_ref):
    h = h_ref[...]
    hb = h.astype(BF16)
    merged = jnp.zeros(h.shape, F32)
    for r, br_ref in enumerate((oa_ref, ob_ref, oc_ref)):
        gpre = _dot(hb, wg_ref[:, r * D_MODEL:(r + 1) * D_MODEL]) + gb_ref[r:r + 1, :]
        merged = merged + jax.nn.sigmoid(gpre) * _dot(br_ref[...], wbr_ref[r])
    mix = _dot(merged.astype(BF16), wout_ref[...])
    o_ref[...] = _layernorm(DEEPNORM_ALPHA * h + mix, g_ref[...], b_ref[...])


def _merge_ln(h, oa, ob, oc, w_gate, gate_bias, w_branch, w_out, g, b):
    t = h.shape[0]
    tm = min(MERGE_TM, t)
    rowspec = pl.BlockSpec((tm, D_MODEL), lambda i: (i, 0))
    const2 = lambda i: (0, 0)
    one = pl.Buffered(1)
    return pl.pallas_call(
        _merge_ln_kernel,
        grid=(t // tm,),
        in_specs=[
            rowspec, rowspec, rowspec, rowspec,
            pl.BlockSpec((D_MODEL, N_BRANCH * D_MODEL), const2, pipeline_mode=one),
            pl.BlockSpec((N_BRANCH, D_MODEL), const2),
            pl.BlockSpec((N_BRANCH, D_MODEL, D_MODEL), lambda i: (0, 0, 0), pipeline_mode=one),
            pl.BlockSpec((D_MODEL, D_MODEL), const2, pipeline_mode=one),
            pl.BlockSpec((1, D_MODEL), const2),
            pl.BlockSpec((1, D_MODEL), const2),
        ],
        out_specs=rowspec,
        out_shape=jax.ShapeDtypeStruct((t, D_MODEL), F32),
        compiler_params=_cparams(("parallel",)),
        name="merge_ln",
    )(h, oa, ob, oc, w_gate, gate_bias, w_branch, w_out, g.reshape(1, D_MODEL), b.reshape(1, D_MODEL))


def _mixer_weights(w_in):
    segs = (A_W, A_W, A_W, D_INNER, CONV_DIM, H_B, C_W, C_W, C_W, N_BRANCH * D_MODEL)
    offs = [0]
    for s in segs:
        offs.append(offs[-1] + s)
    qa, ka, va, zb, xbc, dtb, qc, kc, vc, gate = (w_in[:, offs[n]:offs[n + 1]] for n in range(len(segs)))
    w_qk = jnp.concatenate([qa, ka], axis=1).astype(BF16)
    w_att = jnp.concatenate([va, qc, kc, vc], axis=1).astype(BF16)
    w_ssm = jnp.concatenate([zb, xbc, jnp.pad(dtb, ((0, 0), (0, LANES - H_B)))], axis=1).astype(BF16)
    return w_qk, w_att, w_ssm, gate.astype(BF16)


def kernel(x, ffn1_w_in, ffn1_w_out, ln1_g, ln1_b, w_mix_in, gate_bias, diff_lambda, diff_subln_g,
           ssm_conv_w, ssm_conv_b, ssm_dt_bias, ssm_A_log, ssm_D, ssm_norm_g, w_branch, w_mix_out,
           ln2_g, ln2_b, ffn2_w_in, ffn2_w_out, ln3_g, ln3_b):
    batch, seq, _ = x.shape
    t = batch * seq
    tabs = _rope_tables(seq)
    xf = x.reshape(t, D_MODEL)
    for l in range(DEPTH):
        xf = _ffn_ln(xf, ffn1_w_in[l].astype(BF16), ffn1_w_out[l].astype(BF16), ln1_g[l], ln1_b[l])

        w_qk, w_att, w_ssm, w_gate = _mixer_weights(w_mix_in[l])
        qk, att = _proj_attn(xf, w_qk, w_att, tabs, seq)
        qk = qk.reshape(batch, seq, 2 * A_W)
        att = att.reshape(batch, seq, A_W + 3 * C_W)
        ssm = _proj_ssm(xf, w_ssm).reshape(batch, seq, w_ssm.shape[1])

        o_a = _diff_attn(qk, att, 0, diff_lambda[l], diff_subln_g[l], l, batch, seq)
        nb = A_W // LANES
        o_c = _stick_breaking(att, nb, 2 * nb, 3 * nb, batch, seq)
        o_b = _ssd(ssm, ssm_conv_w[l], ssm_conv_b[l], ssm_dt_bias[l], ssm_A_log[l], ssm_D[l],
                   ssm_norm_g[l], batch, seq)

        xf = _merge_ln(xf, o_a.reshape(t, A_W), o_b.reshape(t, D_INNER), o_c.reshape(t, C_W),
                       w_gate, gate_bias[l], w_branch[l].astype(BF16), w_mix_out[l].astype(BF16),
                       ln2_g[l], ln2_b[l])

        xf = _ffn_ln(xf, ffn2_w_in[l].astype(BF16), ffn2_w_out[l].astype(BF16), ln3_g[l], ln3_b[l])
    return xf.reshape(batch, seq, D_MODEL)
```

```python
import functools
import math

import jax
import jax.numpy as jnp
from jax import lax
from jax.experimental import pallas as pl
from jax.experimental.pallas import tpu as pltpu

F32 = jnp.float32
BF16 = jnp.bfloat16

D_MODEL = 1024
DEPTH = 2
LN_EPS = 1e-5
DEEPNORM_ALPHA = (2 * DEPTH) ** 0.25
D_FF = 2816
FFN_RES = 0.5

CHUNK = 64
H_A = 8
HD_A = 64
ROT_A = HD_A // 4
ROPE_THETA = 500000.0
A_W = H_A * 2 * HD_A

H_B = 16
P_B = 64
D_INNER = H_B * P_B
G_B = 2
N_B = 128
CONV_B = 4
CONV_DIM = D_INNER + 2 * G_B * N_B

H_C = 16
HD_C = 64
C_W = H_C * HD_C
N_BRANCH = 3

LANES = 128
VMEM_LIMIT = 56 * 1024 * 1024

FFN_TM = 512
FFN_CHUNK = 256
PROJ_TM = 512
PROJ_SSM_CHUNKS = 3
MERGE_TM = 512
ATT_A_T = 512
ATT_A_NT = 4
ATT_A_HEADS = 4
ATT_C_T = 512
ATT_C_TK = 256
ATT_C_PAIRS = 4
ATT_C_NT = 4
SSD_L = 256

assert D_FF % FFN_CHUNK == 0
assert ATT_C_T % ATT_C_TK == 0 and ATT_C_NT % (ATT_C_T // ATT_C_TK) == 0
assert HD_A == HD_C

NEG_BIG = -1e30
RUN_DEAD = -160.0
LOG2E = 1.4426950408889634
Q_SCALE = HD_A ** -0.5 * LOG2E


def _cparams(sem):
    return pltpu.CompilerParams(dimension_semantics=sem, vmem_limit_bytes=VMEM_LIMIT)


def _dot(a, b):
    return jnp.dot(a, b, preferred_element_type=F32)


def _dot_nt(a, b):
    return lax.dot_general(a, b, (((1,), (1,)), ((), ())), preferred_element_type=F32)


def _split3(x):
    hi = x.astype(BF16)
    r = x - hi.astype(F32)
    mid = r.astype(BF16)
    lo = (r - mid.astype(F32)).astype(BF16)
    return hi, mid, lo


def _dot_exact_rhs(a_bf16, x):
    hi, mid, lo = _split3(x)
    return _dot(a_bf16, hi) + (_dot(a_bf16, mid) + _dot(a_bf16, lo))


def _dot_exact_lhs(x, a_bf16):
    hi, mid, lo = _split3(x)
    return _dot(hi, a_bf16) + (_dot(mid, a_bf16) + _dot(lo, a_bf16))


def _layernorm(y, g, b):
    mu = jnp.mean(y, axis=-1, keepdims=True)
    yc = y - mu
    var = jnp.mean(yc * yc, axis=-1, keepdims=True)
    return yc * lax.rsqrt(var + LN_EPS) * g + b


def _softplus(x):
    return jnp.maximum(x, 0.0) + jnp.log(1.0 + jnp.exp(-jnp.abs(x)))


def _ffn_ln_kernel(x_ref, win_ref, wout_ref, g_ref, b_ref, o_ref):
    x = x_ref[...]
    xb = x.astype(BF16)
    acc = jnp.zeros(x.shape, F32)
    for c in range(D_FF // FFN_CHUNK):
        lo = c * FFN_CHUNK
        gate = _dot(xb, win_ref[:, lo:lo + FFN_CHUNK])
        up = _dot(xb, win_ref[:, D_FF + lo:D_FF + lo + FFN_CHUNK])
        a = (gate * jax.nn.sigmoid(gate) * up).astype(BF16)
        acc = acc + _dot(a, wout_ref[lo:lo + FFN_CHUNK, :])
    y = DEEPNORM_ALPHA * x + FFN_RES * acc
    o_ref[...] = _layernorm(y, g_ref[...], b_ref[...])


def _ffn_ln(x, w_in, w_out, g, b):
    t = x.shape[0]
    tm = min(FFN_TM, t)
    const = lambda i: (0, 0)
    return pl.pallas_call(
        _ffn_ln_kernel,
        grid=(t // tm,),
        in_specs=[
            pl.BlockSpec((tm, D_MODEL), lambda i: (i, 0)),
            pl.BlockSpec((D_MODEL, 2 * D_FF), const, pipeline_mode=pl.Buffered(1)),
            pl.BlockSpec((D_FF, D_MODEL), const, pipeline_mode=pl.Buffered(1)),
            pl.BlockSpec((1, D_MODEL), const),
            pl.BlockSpec((1, D_MODEL), const),
        ],
        out_specs=pl.BlockSpec((tm, D_MODEL), lambda i: (i, 0)),
        out_shape=jax.ShapeDtypeStruct((t, D_MODEL), F32),
        compiler_params=_cparams(("parallel",)),
        name="ffn_ln",
    )(x, w_in, w_out, g.reshape(1, D_MODEL), b.reshape(1, D_MODEL))


def _proj_ssm_kernel(x_ref, w_ref, o_ref):
    xb = x_ref[...].astype(BF16)
    n = w_ref.shape[1]
    step = n // PROJ_SSM_CHUNKS
    for lo in range(0, n, step):
        o_ref[:, lo:lo + step] = _dot(xb, w_ref[:, lo:lo + step])


def _proj_ssm(x, w):
    t = x.shape[0]
    n = w.shape[1]
    tm = min(PROJ_TM, t)
    return pl.pallas_call(
        _proj_ssm_kernel,
        grid=(t // tm,),
        in_specs=[
            pl.BlockSpec((tm, D_MODEL), lambda i: (i, 0)),
            pl.BlockSpec((D_MODEL, n), lambda i: (0, 0), pipeline_mode=pl.Buffered(1)),
        ],
        out_specs=pl.BlockSpec((tm, n), lambda i: (i, 0)),
        out_shape=jax.ShapeDtypeStruct((t, n), F32),
        compiler_params=_cparams(("parallel",)),
        name="proj_ssm",
    )(x, w)


def _proj_attn_kernel(x_ref, wqk_ref, watt_ref, c_ref, s1_ref, s2_ref, qk_ref, att_ref):
    xb = x_ref[...].astype(BF16)
    for blk in range(2):
        acc = _dot(xb, wqk_ref[:, blk * A_W:(blk + 1) * A_W])
        scale = Q_SCALE if blk == 0 else 1.0
        c = c_ref[...] * scale
        s1 = s1_ref[...] * scale
        s2 = s2_ref[...] * scale
        for h in range(A_W // LANES):
            a = acc[:, h * LANES:(h + 1) * LANES]
            r = a * c + pltpu.roll(a, ROT_A // 2, axis=1) * s1 + pltpu.roll(a, LANES - ROT_A // 2, axis=1) * s2
            qk_ref[:, blk * A_W + h * LANES:blk * A_W + (h + 1) * LANES] = r.astype(qk_ref.dtype)
    for blk in range(watt_ref.shape[1] // D_MODEL):
        acc = _dot(xb, watt_ref[:, blk * D_MODEL:(blk + 1) * D_MODEL])
        if blk == A_W // D_MODEL:
            acc = acc * Q_SCALE
        att_ref[:, blk * D_MODEL:(blk + 1) * D_MODEL] = acc.astype(att_ref.dtype)


def _proj_attn(x, w_qk, w_att, tabs, seq):
    t = x.shape[0]
    tm = min(PROJ_TM, seq)
    nrow = seq // tm
    tab_spec = pl.BlockSpec((tm, LANES), lambda i: (i % nrow, 0))
    const = lambda i: (0, 0)
    nqk, natt = w_qk.shape[1], w_att.shape[1]
    return pl.pallas_call(
        _proj_attn_kernel,
        grid=(t // tm,),
        in_specs=[
            pl.BlockSpec((tm, D_MODEL), lambda i: (i, 0)),
            pl.BlockSpec((D_MODEL, nqk), const, pipeline_mode=pl.Buffered(1)),
            pl.BlockSpec((D_MODEL, natt), const, pipeline_mode=pl.Buffered(1)),
            tab_spec, tab_spec, tab_spec,
        ],
        out_specs=[pl.BlockSpec((tm, nqk), lambda i: (i, 0)), pl.BlockSpec((tm, natt), lambda i: (i, 0))],
        out_shape=[jax.ShapeDtypeStruct((t, nqk), BF16), jax.ShapeDtypeStruct((t, natt), BF16)],
        compiler_params=_cparams(("parallel",)),
        name="proj_attn",
    )(x, w_qk, w_att, *tabs)


def _rope_tables(seq):
    half = ROT_A // 2
    inv_freq = ROPE_THETA ** (-jnp.arange(half, dtype=F32) * 2.0 / ROT_A)
    ang = jnp.arange(seq, dtype=F32)[:, None] * inv_freq[None, :]
    cos, sin = jnp.cos(ang), jnp.sin(ang)
    ones = jnp.ones((seq, HD_A - ROT_A), F32)
    zeros = jnp.zeros((seq, HD_A - ROT_A), F32)
    zh = jnp.zeros((seq, half), F32)
    c = jnp.concatenate([cos, cos, ones], axis=1)
    s1 = jnp.concatenate([zh, sin, zeros], axis=1)
    s2 = jnp.concatenate([-sin, zh, zeros], axis=1)
    rep = LANES // HD_A
    return tuple(jnp.tile(v, (1, rep)) for v in (c, s1, s2))


def _diff_attn_kernel(q_ref, k_ref, v_ref, lam_ref, g_ref, o_ref, qq_ref, m_ref, l_ref, acc_ref, *, lam_init):
    t = q_ref.shape[0]
    nh = q_ref.shape[1] // LANES
    i = pl.program_id(2)
    lane = lax.broadcasted_iota(jnp.int32, (t, LANES), 1)
    for hh in range(nh):
        q = q_ref[:, hh * LANES:(hh + 1) * LANES]
        zero = jnp.zeros_like(q)
        qq_ref[hh] = jnp.concatenate([jnp.where(lane < HD_A, q, zero), jnp.where(lane < HD_A, zero, q)], axis=0)
    m_ref[...] = jnp.full(m_ref.shape, NEG_BIG, F32)
    l_ref[...] = jnp.zeros(l_ref.shape, F32)
    acc_ref[...] = jnp.zeros(acc_ref.shape, F32)

    def step(j, n, masked):
        start = pl.multiple_of(j * t, t)
        for hh in range(nh):
            kb = k_ref[pl.ds(start, n * t), hh * LANES:(hh + 1) * LANES]
            vb = v_ref[pl.ds(start, n * t), hh * LANES:(hh + 1) * LANES]
            m = m_ref[hh]
            s = _dot_nt(qq_ref[hh], kb)
            if masked:
                qchunk = lax.broadcasted_iota(jnp.int32, (2 * t, t), 0) % t // CHUNK
                kchunk = lax.broadcasted_iota(jnp.int32, (2 * t, t), 1) // CHUNK
                s = jnp.where(kchunk <= qchunk, s, NEG_BIG)
            m_new = jnp.maximum(m, jnp.max(s, axis=1, keepdims=True))
            p = jnp.concatenate([jnp.exp2(s[:, c:c + LANES] - m_new) for c in range(0, n * t, LANES)], axis=1)
            corr = jnp.exp2(m - m_new)
            l_ref[hh] = corr * l_ref[hh] + jnp.sum(p, axis=1, keepdims=True)
            acc_ref[hh] = corr * acc_ref[hh] + _dot(p.astype(BF16), vb)
            m_ref[hh] = m_new

    done = 0
    n = ATT_A_NT
    while n >= 1:
        cnt = (i - done) // n

        def body(g, c, n=n, done=done):
            step(done + g * n, n, False)
            return c

        lax.fori_loop(0, cnt, body, 0)
        done = done + cnt * n
        n //= 2
    step(i, 1, True)
    lf = lam_ref[...]
    lam_full = (jnp.exp(jnp.sum(lf[0:1] * lf[1:2], axis=1, keepdims=True))
                - jnp.exp(jnp.sum(lf[2:3] * lf[3:4], axis=1, keepdims=True)) + lam_init)
    for hh in range(nh):
        o = acc_ref[hh] / l_ref[hh]
        d = o[:t] - lam_full * o[t:]
        y = d * lax.rsqrt(jnp.mean(d * d, axis=-1, keepdims=True) + LN_EPS) * g_ref[...]
        o_ref[:, hh * LANES:(hh + 1) * LANES] = (y * (1.0 - lam_init)).astype(o_ref.dtype)


def _diff_attn(qk, v_arr, v_col0, lam, subln_g, layer, batch, seq):
    t = min(ATT_A_T, seq)
    nh = ATT_A_HEADS
    w = nh * LANES
    assert H_A % nh == 0 and v_col0 % nh == 0
    lam_init = 0.8 - 0.6 * math.exp(-0.3 * layer)
    kern = functools.partial(_diff_attn_kernel, lam_init=lam_init)
    return pl.pallas_call(
        kern,
        grid=(batch, H_A // nh, seq // t),
        in_specs=[
            pl.BlockSpec((None, t, w), lambda b, h, i: (b, i, h)),
            pl.BlockSpec((None, seq, w), lambda b, h, i: (b, 0, H_A // nh + h)),
            pl.BlockSpec((None, seq, w), lambda b, h, i: (b, 0, v_col0 // nh + h)),
            pl.BlockSpec((4, HD_A), lambda b, h, i: (0, 0)),
            pl.BlockSpec((1, 2 * HD_A), lambda b, h, i: (0, 0)),
        ],
        out_specs=pl.BlockSpec((None, t, w), lambda b, h, i: (b, i, h)),
        out_shape=jax.ShapeDtypeStruct((batch, seq, A_W), BF16),
        scratch_shapes=[pltpu.VMEM((nh, 2 * t, LANES), BF16)] + [pltpu.VMEM((nh, 2 * t, LANES), F32)] * 3,
        compiler_params=_cparams(("parallel", "parallel", "parallel")),
        name="diff_attn",
    )(qk, qk, v_arr, lam, subln_g.reshape(1, 2 * HD_A))


def _stick_kernel(q_ref, k_ref, v_ref, o_ref, qq_ref, run_ref, acc_ref):
    tq = q_ref.shape[0]
    tk = min(ATT_C_TK, tq)
    per = tq // tk
    i = pl.program_id(2)
    npair = q_ref.shape[1] // LANES
    lane = lax.broadcasted_iota(jnp.int32, (tq, LANES), 1)
    for hp in range(npair):
        q = q_ref[:, hp * LANES:(hp + 1) * LANES]
        zero = jnp.zeros_like(q)
        qq_ref[hp] = jnp.concatenate([jnp.where(lane < HD_C, q, zero), jnp.where(lane < HD_C, zero, q)], axis=0)
    u = jnp.where(lax.broadcasted_iota(jnp.int32, (tk, tk), 0) >= lax.broadcasted_iota(jnp.int32, (tk, tk), 1),
                  -1.0, 0.0).astype(BF16)

    def front(hp, qs, j, tri):
        kb = k_ref[pl.ds(pl.multiple_of(j * tk, tk), tk), hp * LANES:(hp + 1) * LANES]
        z = _dot_nt(qs, kb)
        neg_abs = lax.bitcast_convert_type(
            lax.bitcast_convert_type(z, jnp.uint32) | jnp.uint32(0x80000000), F32)
        sp = jnp.maximum(z, 0.0) + jnp.log2(1.0 + jnp.exp2(neg_abs))
        if tri is not None:
            sp = jnp.where(tri, sp, 0.0)
        tail = _dot(sp.astype(BF16), u)
        return z + tail, tail[:, 0:1]

    def back(hp, j, w, run, tri):
        a = jnp.concatenate([jnp.exp2(w[:, c:c + LANES] + run) for c in range(0, tk, LANES)], axis=1)
        if tri is not None:
            a = jnp.where(tri, a, 0.0)
        return _dot(a.astype(BF16), v_ref[pl.ds(pl.multiple_of(j * tk, tk), tk), hp * LANES:(hp + 1) * LANES])

    def group(hp, qs, js, tris, run, acc, gates=None):
        fronts = [front(hp, qs, j, tri) for j, tri in zip(js, tris)]
        for n, (j, tri, (w, mass)) in enumerate(zip(js, tris, fronts)):
            pv = back(hp, j, w, run, tri)
            if gates is not None and gates[n] is not None:
                pv, mass = gates[n] * pv, gates[n] * mass
            acc = acc + pv
            run = run + mass
        return run, acc

    def full_group(first, n):
        for hp in range(npair):
            run, acc = group(hp, qq_ref[hp], [first - m for m in range(n)], [None] * n, run_ref[hp], acc_ref[hp])
            run_ref[hp] = run
            acc_ref[hp] = acc

    tri = (lax.broadcasted_iota(jnp.int32, (2 * tk, tk), 1)
           < lax.broadcasted_iota(jnp.int32, (2 * tk, tk), 0) % tk)
    zeros = jnp.zeros((2 * tk, LANES), F32)
    below = jnp.maximum(per * i - 1, 0)
    has_below = (i > 0).astype(F32)
    for hp in range(npair):
        for r in range(per):
            even, odd = pl.ds(r * tk, tk), pl.ds(tq + r * tk, tk)
            qs = jnp.concatenate([qq_ref[hp, even, :], qq_ref[hp, odd, :]], axis=0)
            run, acc = group(hp, qs, [per * i + r - m for m in range(r + 1)] + [below], [tri] + [None] * (r + 1),
                             zeros, zeros, [None] * (r + 1) + [has_below])
            run_ref[hp, even, :], run_ref[hp, odd, :] = run[:tk], run[tk:]
            acc_ref[hp, even, :], acc_ref[hp, odd, :] = acc[:tk], acc[tk:]

    def alive():
        return (jnp.max(run_ref[...]) > RUN_DEAD).astype(jnp.int32)

    def walk(first, n, live):
        count = (first + 1) // n

        def cond(c):
            g, live = c
            return jnp.logical_and(g < count, live > 0)

        def body(c):
            g, _ = c
            full_group(first - n * g, n)
            return g + 1, alive()

        g, live = lax.while_loop(cond, body, (jnp.int32(0), live))
        return first - n * g, live

    nxt, live = per * i - 2, alive()
    n = ATT_C_NT
    while n >= 1:
        nxt, live = walk(nxt, n, live)
        n //= 2
    for hp in range(npair):
        o_ref[:, hp * LANES:(hp + 1) * LANES] = jnp.where(
            lane < HD_C, acc_ref[hp, 0:tq, :], acc_ref[hp, tq:2 * tq, :]).astype(o_ref.dtype)


def _stick_breaking(arr, q_col0, k_col0, v_col0, batch, seq):
    t = min(ATT_C_T, seq)
    npair = ATT_C_PAIRS
    w = npair * LANES
    nhp = C_W // w
    assert C_W % w == 0 and q_col0 % npair == 0 and k_col0 % npair == 0 and v_col0 % npair == 0
    return pl.pallas_call(
        _stick_kernel,
        grid=(batch, nhp, seq // t),
        in_specs=[
            pl.BlockSpec((None, t, w), lambda b, h, i: (b, i, q_col0 // npair + h)),
            pl.BlockSpec((None, seq, w), lambda b, h, i: (b, 0, k_col0 // npair + h)),
            pl.BlockSpec((None, seq, w), lambda b, h, i: (b, 0, v_col0 // npair + h)),
        ],
        out_specs=pl.BlockSpec((None, t, w), lambda b, h, i: (b, i, h)),
        out_shape=jax.ShapeDtypeStruct((batch, seq, C_W), BF16),
        scratch_shapes=[pltpu.VMEM((npair, 2 * t, LANES), BF16)] + [pltpu.VMEM((npair, 2 * t, LANES), F32)] * 2,
        compiler_params=_cparams(("parallel", "parallel", "parallel")),
        name="stick_breaking",
    )(arr, arr, arr)


HALO = 8


def _ssd_kernel(z_ref, x_ref, bc_ref, dt_ref, cw_ref, cb_ref, dtb_ref, alog_ref, dskip_ref, ng_ref,
                o_ref, xe_ref, st_ref):
    L = z_ref.shape[0]
    c = pl.program_id(1)

    @pl.when(c == 0)
    def _():
        xe_ref[0:HALO, :] = jnp.zeros((HALO, CONV_DIM), F32)
        st_ref[...] = jnp.zeros(st_ref.shape, F32)

    xe_ref[HALO:HALO + L, 0:D_INNER] = x_ref[...]
    xe_ref[HALO:HALO + L, D_INNER:CONV_DIM] = bc_ref[...]
    conv = cb_ref[...]
    for r in range(CONV_B):
        off = HALO - (CONV_B - 1) + r
        conv = conv + cw_ref[r:r + 1, :] * xe_ref[off:off + L, :]
    xe_ref[0:HALO, :] = xe_ref[L:L + HALO, :]
    xc = conv * jax.nn.sigmoid(conv)
    xs = xc[:, :D_INNER]

    dt = _softplus(dt_ref[...] + dtb_ref[...])
    a = dt * (-jnp.exp(alog_ref[...]))
    row_i = lax.broadcasted_iota(jnp.int32, (L, L), 0)
    col_i = lax.broadcasted_iota(jnp.int32, (L, L), 1)
    causal = col_i <= row_i
    tri = jnp.where(causal, 1.0, 0.0).astype(BF16)
    acum = _dot_exact_rhs(tri, a)
    acum_t = acum.T
    a_last = acum[L - 1:L, :]
    e_in = jnp.exp(acum)
    e_out = jnp.exp(a_last - acum)

    expand = jnp.where(lax.broadcasted_iota(jnp.int32, (LANES, D_INNER), 0)
                       == lax.broadcasted_iota(jnp.int32, (LANES, D_INNER), 1) // P_B,
                       1.0, 0.0).astype(BF16)
    dt_x = _dot_exact_lhs(dt, expand)
    e_in_x = _dot_exact_lhs(e_in, expand)
    e_out_x = _dot_exact_lhs(e_out, expand)

    xdt = xs * dt_x
    xdt_b = xdt.astype(BF16)
    xdt_out_b = (xdt * e_out_x).astype(BF16)
    lane = lax.broadcasted_iota(jnp.int32, (L, LANES), 1)

    hg = H_B // G_B
    gw = D_INNER // G_B
    y_parts = []
    for g in range(G_B):
        bm_f = xc[:, D_INNER + g * N_B:D_INNER + (g + 1) * N_B]
        bm = bm_f.astype(BF16)
        bm_t = bm_f.T.astype(BF16)
        cm = xc[:, D_INNER + G_B * N_B + g * N_B:D_INNER + G_B * N_B + (g + 1) * N_B].astype(BF16)
        cb = _dot_nt(cm, bm)
        st_g = st_ref[:, g * gw:(g + 1) * gw]
        y_off = _dot(cm, st_g.astype(BF16)) * e_in_x[:, g * gw:(g + 1) * gw]
        y_diag = []
        for hp in range(hg // 2):
            h0 = g * hg + 2 * hp
            pair = xdt_b[:, h0 * P_B:(h0 + 2) * P_B]
            ys = []
            for h in (h0, h0 + 1):
                seg = acum[:, h:h + 1] - acum_t[h:h + 1, :]
                m = (cb * jnp.exp(jnp.where(causal, seg, -jnp.inf))).astype(BF16)
                ys.append(_dot(m, pair))
            y_diag.append(jnp.where(lane < P_B, ys[0], ys[1]))
        y_parts.append(jnp.concatenate(y_diag, axis=1) + y_off)
        upd = _dot(bm_t, xdt_out_b[:, g * gw:(g + 1) * gw])
        st_ref[:, g * gw:(g + 1) * gw] = st_g * e_in_x[L - 1:L, g * gw:(g + 1) * gw] + upd

    y = jnp.concatenate(y_parts, axis=1) + dskip_ref[...] * xs
    zz = z_ref[...]
    y = y * (zz * jax.nn.sigmoid(zz))
    outs = []
    for g in range(G_B):
        yg = y[:, g * gw:(g + 1) * gw]
        outs.append(yg * lax.rsqrt(jnp.mean(yg * yg, axis=-1, keepdims=True) + LN_EPS))
    o_ref[...] = (jnp.concatenate(outs, axis=1) * ng_ref[...]).astype(o_ref.dtype)


def _ssd(ssm, conv_w, conv_b, dt_bias, a_log, d_skip, norm_g, batch, seq):
    L = min(SSD_L, seq)
    pad = LANES - H_B
    row = lambda v: v.reshape(1, -1).astype(F32)
    dtb = jnp.pad(row(dt_bias), ((0, 0), (0, pad)))
    alog = jnp.pad(row(a_log), ((0, 0), (0, pad)))
    dskip = jnp.repeat(row(d_skip), P_B, axis=1)
    const = lambda b, c: (0, 0)
    bc_w = CONV_DIM - D_INNER
    return pl.pallas_call(
        _ssd_kernel,
        grid=(batch, seq // L),
        in_specs=[
            pl.BlockSpec((None, L, D_INNER), lambda b, c: (b, c, 0)),
            pl.BlockSpec((None, L, D_INNER), lambda b, c: (b, c, 1)),
            pl.BlockSpec((None, L, bc_w), lambda b, c: (b, c, 2 * D_INNER // bc_w)),
            pl.BlockSpec((None, L, LANES), lambda b, c: (b, c, (D_INNER + CONV_DIM) // LANES)),
            pl.BlockSpec((CONV_B, CONV_DIM), const),
            pl.BlockSpec((1, CONV_DIM), const),
            pl.BlockSpec((1, LANES), const),
            pl.BlockSpec((1, LANES), const),
            pl.BlockSpec((1, D_INNER), const),
            pl.BlockSpec((1, D_INNER), const),
        ],
        out_specs=pl.BlockSpec((None, L, D_INNER), lambda b, c: (b, c, 0)),
        out_shape=jax.ShapeDtypeStruct((batch, seq, D_INNER), BF16),
        scratch_shapes=[pltpu.VMEM((HALO + L, CONV_DIM), F32), pltpu.VMEM((N_B, D_INNER), F32)],
        compiler_params=_cparams(("parallel", "arbitrary")),
        name="ssd",
    )(ssm, ssm, ssm, ssm, conv_w, row(conv_b), dtb, alog, dskip, row(norm_g))


def _merge_ln_kernel(h_ref, oa_ref, ob_ref, oc_ref, wg_ref, gb_ref, wbr_ref, wout_ref, g_ref, b_ref, o_ref):
    h = h_ref[...]
    hb = h.astype(BF16)
    merged = jnp.zeros(h.shape, F32)
    for r, br_ref in enumerate((oa_ref, ob_ref, oc_ref)):
        gpre = _dot(hb, wg_ref[:, r * D_MODEL:(r + 1) * D_MODEL]) + gb_ref[r:r + 1, :]
        merged = merged + jax.nn.sigmoid(gpre) * _dot(br_ref[...], wbr_ref[r])
    mix = _dot(merged.astype(BF16), wout_ref[...])
    o_ref[...] = _layernorm(DEEPNORM_ALPHA * h + mix, g_ref[...], b_ref[...])


def _merge_ln(h, oa, ob, oc, w_gate, gate_bias, w_branch, w_out, g, b):
    t = h.shape[0]
    tm = min(MERGE_TM, t)
    rowspec = pl.BlockSpec((tm, D_MODEL), lambda i: (i, 0))
    const2 = lambda i: (0, 0)
    one = pl.Buffered(1)
    return pl.pallas_call(
        _merge_ln_kernel,
        grid=(t // tm,),
        in_specs=[
            rowspec, rowspec, rowspec, rowspec,
            pl.BlockSpec((D_MODEL, N_BRANCH * D_MODEL), const2, pipeline_mode=one),
            pl.BlockSpec((N_BRANCH, D_MODEL), const2),
            pl.BlockSpec((N_BRANCH, D_MODEL, D_MODEL), lambda i: (0, 0, 0), pipeline_mode=one),
            pl.BlockSpec((D_MODEL, D_MODEL), const2, pipeline_mode=one),
            pl.BlockSpec((1, D_MODEL), const2),
            pl.BlockSpec((1, D_MODEL), const2),
        ],
        out_specs=rowspec,
        out_shape=jax.ShapeDtypeStruct((t, D_MODEL), F32),
        compiler_params=_cparams(("parallel",)),
        name="merge_ln",
    )(h, oa, ob, oc, w_gate, gate_bias, w_branch, w_out, g.reshape(1, D_MODEL), b.reshape(1, D_MODEL))


def _mixer_weights(w_in):
    segs = (A_W, A_W, A_W, D_INNER, CONV_DIM, H_B, C_W, C_W, C_W, N_BRANCH * D_MODEL)
    offs = [0]
    for s in segs:
        offs.append(offs[-1] + s)
    qa, ka, va, zb, xbc, dtb, qc, kc, vc, gate = (w_in[:, offs[n]:offs[n + 1]] for n in range(len(segs)))
    w_qk = jnp.concatenate([qa, ka], axis=1).astype(BF16)
    w_att = jnp.concatenate([va, qc, kc, vc], axis=1).astype(BF16)
    w_ssm = jnp.concatenate([zb, xbc, jnp.pad(dtb, ((0, 0), (0, LANES - H_B)))], axis=1).astype(BF16)
    return w_qk, w_att, w_ssm, gate.astype(BF16)


def kernel(x, ffn1_w_in, ffn1_w_out, ln1_g, ln1_b, w_mix_in, gate_bias, diff_lambda, diff_subln_g,
           ssm_conv_w, ssm_conv_b, ssm_dt_bias, ssm_A_log, ssm_D, ssm_norm_g, w_branch, w_mix_out,
           ln2_g, ln2_b, ffn2_w_in, ffn2_w_out, ln3_g, ln3_b):
    batch, seq, _ = x.shape
    t = batch * seq
    tabs = _rope_tables(seq)
    xf = x.reshape(t, D_MODEL)
    for l in range(DEPTH):
        xf = _ffn_ln(xf, ffn1_w_in[l].astype(BF16), ffn1_w_out[l].astype(BF16), ln1_g[l], ln1_b[l])

        w_qk, w_att, w_ssm, w_gate = _mixer_weights(w_mix_in[l])
        qk, att = _proj_attn(xf, w_qk, w_att, tabs, seq)
        qk = qk.reshape(batch, seq, 2 * A_W)
        att = att.reshape(batch, seq, A_W + 3 * C_W)
        ssm = _proj_ssm(xf, w_ssm).reshape(batch, seq, w_ssm.shape[1])

        o_a = _diff_attn(qk, att, 0, diff_lambda[l], diff_subln_g[l], l, batch, seq)
        nb = A_W // LANES
        o_c = _stick_breaking(att, nb, 2 * nb, 3 * nb, batch, seq)
        o_b = _ssd(ssm, ssm_conv_w[l], ssm_conv_b[l], ssm_dt_bias[l], ssm_A_log[l], ssm_D[l],
                   ssm_norm_g[l], batch, seq)

        xf = _merge_ln(xf, o_a.reshape(t, A_W), o_b.reshape(t, D_INNER), o_c.reshape(t, C_W),
                       w_gate, gate_bias[l], w_branch[l].astype(BF16), w_mix_out[l].astype(BF16),
                       ln2_g[l], ln2_b[l])

        xf = _ffn_ln(xf, ffn2_w_in[l].astype(BF16), ffn2_w_out[l].astype(BF16), ln3_g[l], ln3_b[l])
    return xf.reshape(batch, seq, D_MODEL)
```

```python
import functools
import math

import jax
import jax.numpy as jnp
from jax import lax
from jax.experimental import pallas as pl
from jax.experimental.pallas import tpu as pltpu

F32 = jnp.float32
BF16 = jnp.bfloat16

D_MODEL = 1024
DEPTH = 2
LN_EPS = 1e-5
DEEPNORM_ALPHA = (2 * DEPTH) ** 0.25
D_FF = 2816
FFN_RES = 0.5

CHUNK = 64
H_A = 8
HD_A = 64
ROT_A = HD_A // 4
ROPE_THETA = 500000.0
A_W = H_A * 2 * HD_A

H_B = 16
P_B = 64
D_INNER = H_B * P_B
G_B = 2
N_B = 128
CONV_B = 4
CONV_DIM = D_INNER + 2 * G_B * N_B

H_C = 16
HD_C = 64
C_W = H_C * HD_C
N_BRANCH = 3

LANES = 128
VMEM_LIMIT = 56 * 1024 * 1024

FFN_TM = 512
FFN_CHUNK = 256
PROJ_TM = 512
PROJ_SSM_CHUNKS = 3
MERGE_TM = 512
ATT_A_T = 512
ATT_A_NT = 4
ATT_A_HEADS = 4
ATT_C_T = 512
ATT_C_TK = 256
ATT_C_PAIRS = 4
ATT_C_NT = 4
SSD_L = 256

assert D_FF % FFN_CHUNK == 0
assert ATT_C_T % ATT_C_TK == 0 and ATT_C_NT % (ATT_C_T // ATT_C_TK) == 0
assert HD_A == HD_C

NEG_BIG = -1e30
RUN_DEAD = -160.0
LOG2E = 1.4426950408889634
Q_SCALE = HD_A ** -0.5 * LOG2E


def _cparams(sem, fuse_inputs=None):
    return pltpu.CompilerParams(dimension_semantics=sem, vmem_limit_bytes=VMEM_LIMIT,
                                allow_input_fusion=fuse_inputs)


def _dot(a, b):
    return jnp.dot(a, b, preferred_element_type=F32)


def _dot_nt(a, b):
    return lax.dot_general(a, b, (((1,), (1,)), ((), ())), preferred_element_type=F32)


def _split3(x):
    hi = x.astype(BF16)
    r = x - hi.astype(F32)
    mid = r.astype(BF16)
    lo = (r - mid.astype(F32)).astype(BF16)
    return hi, mid, lo


def _dot_exact_rhs(a_bf16, x):
    hi, mid, lo = _split3(x)
    return _dot(a_bf16, hi) + (_dot(a_bf16, mid) + _dot(a_bf16, lo))


def _dot_exact_lhs(x, a_bf16):
    hi, mid, lo = _split3(x)
    return _dot(hi, a_bf16) + (_dot(mid, a_bf16) + _dot(lo, a_bf16))


def _layernorm(y, g, b):
    mu = jnp.mean(y, axis=-1, keepdims=True)
    yc = y - mu
    var = jnp.mean(yc * yc, axis=-1, keepdims=True)
    return yc * lax.rsqrt(var + LN_EPS) * g + b


def _softplus(x):
    return jnp.maximum(x, 0.0) + jnp.log(1.0 + jnp.exp(-jnp.abs(x)))


def _ffn_ln_kernel(x_ref, win_ref, wout_ref, g_ref, b_ref, o_ref):
    x = x_ref[...]
    xb = x.astype(BF16)
    acc = jnp.zeros(x.shape, F32)
    for c in range(D_FF // FFN_CHUNK):
        lo = c * FFN_CHUNK
        gate = _dot(xb, win_ref[:, lo:lo + FFN_CHUNK])
        up = _dot(xb, win_ref[:, D_FF + lo:D_FF + lo + FFN_CHUNK])
        a = (gate * jax.nn.sigmoid(gate) * up).astype(BF16)
        acc = acc + _dot(a, wout_ref[lo:lo + FFN_CHUNK, :])
    y = DEEPNORM_ALPHA * x + FFN_RES * acc
    o_ref[...] = _layernorm(y, g_ref[...], b_ref[...])


def _ffn_ln(x, w_in, w_out, g, b):
    t = x.shape[0]
    tm = min(FFN_TM, t)
    const = lambda i: (0, 0)
    return pl.pallas_call(
        _ffn_ln_kernel,
        grid=(t // tm,),
        in_specs=[
            pl.BlockSpec((tm, D_MODEL), lambda i: (i, 0)),
            pl.BlockSpec((D_MODEL, 2 * D_FF), const, pipeline_mode=pl.Buffered(1)),
            pl.BlockSpec((D_FF, D_MODEL), const, pipeline_mode=pl.Buffered(1)),
            pl.BlockSpec((1, D_MODEL), const),
            pl.BlockSpec((1, D_MODEL), const),
        ],
        out_specs=pl.BlockSpec((tm, D_MODEL), lambda i: (i, 0)),
        out_shape=jax.ShapeDtypeStruct((t, D_MODEL), F32),
        compiler_params=_cparams(("parallel",), fuse_inputs=[False, True, True, False, False]),
        name="ffn_ln",
    )(x, w_in, w_out, g.reshape(1, D_MODEL), b.reshape(1, D_MODEL))


def _proj_ssm_kernel(x_ref, w_ref, o_ref):
    xb = x_ref[...].astype(BF16)
    n = w_ref.shape[1]
    step = n // PROJ_SSM_CHUNKS
    for lo in range(0, n, step):
        o_ref[:, lo:lo + step] = _dot(xb, w_ref[:, lo:lo + step])


def _proj_ssm(x, w):
    t = x.shape[0]
    n = w.shape[1]
    tm = min(PROJ_TM, t)
    return pl.pallas_call(
        _proj_ssm_kernel,
        grid=(t // tm,),
        in_specs=[
            pl.BlockSpec((tm, D_MODEL), lambda i: (i, 0)),
            pl.BlockSpec((D_MODEL, n), lambda i: (0, 0), pipeline_mode=pl.Buffered(1)),
        ],
        out_specs=pl.BlockSpec((tm, n), lambda i: (i, 0)),
        out_shape=jax.ShapeDtypeStruct((t, n), F32),
        compiler_params=_cparams(("parallel",)),
        name="proj_ssm",
    )(x, w)


def _proj_attn_kernel(x_ref, wqk_ref, watt_ref, c_ref, s1_ref, s2_ref, qk_ref, att_ref):
    xb = x_ref[...].astype(BF16)
    for blk in range(2):
        acc = _dot(xb, wqk_ref[:, blk * A_W:(blk + 1) * A_W])
        scale = Q_SCALE if blk == 0 else 1.0
        c = c_ref[...] * scale
        s1 = s1_ref[...] * scale
        s2 = s2_ref[...] * scale
        for h in range(A_W // LANES):
            a = acc[:, h * LANES:(h + 1) * LANES]
            r = a * c + pltpu.roll(a, ROT_A // 2, axis=1) * s1 + pltpu.roll(a, LANES - ROT_A // 2, axis=1) * s2
            qk_ref[:, blk * A_W + h * LANES:blk * A_W + (h + 1) * LANES] = r.astype(qk_ref.dtype)
    for blk in range(watt_ref.shape[1] // D_MODEL):
        acc = _dot(xb, watt_ref[:, blk * D_MODEL:(blk + 1) * D_MODEL])
        if blk == A_W // D_MODEL:
            acc = acc * Q_SCALE
        att_ref[:, blk * D_MODEL:(blk + 1) * D_MODEL] = acc.astype(att_ref.dtype)


def _proj_attn(x, w_qk, w_att, tabs, seq):
    t = x.shape[0]
    tm = min(PROJ_TM, seq)
    nrow = seq // tm
    tab_spec = pl.BlockSpec((tm, LANES), lambda i: (i % nrow, 0))
    const = lambda i: (0, 0)
    nqk, natt = w_qk.shape[1], w_att.shape[1]
    return pl.pallas_call(
        _proj_attn_kernel,
        grid=(t // tm,),
        in_specs=[
            pl.BlockSpec((tm, D_MODEL), lambda i: (i, 0)),
            pl.BlockSpec((D_MODEL, nqk), const, pipeline_mode=pl.Buffered(1)),
            pl.BlockSpec((D_MODEL, natt), const, pipeline_mode=pl.Buffered(1)),
            tab_spec, tab_spec, tab_spec,
        ],
        out_specs=[pl.BlockSpec((tm, nqk), lambda i: (i, 0)), pl.BlockSpec((tm, natt), lambda i: (i, 0))],
        out_shape=[jax.ShapeDtypeStruct((t, nqk), BF16), jax.ShapeDtypeStruct((t, natt), BF16)],
        compiler_params=_cparams(("parallel",)),
        name="proj_attn",
    )(x, w_qk, w_att, *tabs)


def _rope_tables(seq):
    half = ROT_A // 2
    inv_freq = ROPE_THETA ** (-jnp.arange(half, dtype=F32) * 2.0 / ROT_A)
    ang = jnp.arange(seq, dtype=F32)[:, None] * inv_freq[None, :]
    cos, sin = jnp.cos(ang), jnp.sin(ang)
    ones = jnp.ones((seq, HD_A - ROT_A), F32)
    zeros = jnp.zeros((seq, HD_A - ROT_A), F32)
    zh = jnp.zeros((seq, half), F32)
    c = jnp.concatenate([cos, cos, ones], axis=1)
    s1 = jnp.concatenate([zh, sin, zeros], axis=1)
    s2 = jnp.concatenate([-sin, zh, zeros], axis=1)
    rep = LANES // HD_A
    return tuple(jnp.tile(v, (1, rep)) for v in (c, s1, s2))


def _diff_attn_kernel(q_ref, k_ref, v_ref, lam_ref, g_ref, o_ref, qq_ref, m_ref, l_ref, acc_ref, *, lam_init):
    t = q_ref.shape[0]
    nh = q_ref.shape[1] // LANES
    i = pl.program_id(2)
    lane = lax.broadcasted_iota(jnp.int32, (t, LANES), 1)
    for hh in range(nh):
        q = q_ref[:, hh * LANES:(hh + 1) * LANES]
        zero = jnp.zeros_like(q)
        qq_ref[hh] = jnp.concatenate([jnp.where(lane < HD_A, q, zero), jnp.where(lane < HD_A, zero, q)], axis=0)
    m_ref[...] = jnp.full(m_ref.shape, NEG_BIG, F32)
    l_ref[...] = jnp.zeros(l_ref.shape, F32)
    acc_ref[...] = jnp.zeros(acc_ref.shape, F32)

    def step(j, n, masked):
        start = pl.multiple_of(j * t, t)
        for hh in range(nh):
            kb = k_ref[pl.ds(start, n * t), hh * LANES:(hh + 1) * LANES]
            vb = v_ref[pl.ds(start, n * t), hh * LANES:(hh + 1) * LANES]
            m = m_ref[hh]
            s = _dot_nt(qq_ref[hh], kb)
            if masked:
                qchunk = lax.broadcasted_iota(jnp.int32, (2 * t, t), 0) % t // CHUNK
                kchunk = lax.broadcasted_iota(jnp.int32, (2 * t, t), 1) // CHUNK
                s = jnp.where(kchunk <= qchunk, s, NEG_BIG)
            m_new = jnp.maximum(m, jnp.max(s, axis=1, keepdims=True))
            p = jnp.concatenate([jnp.exp2(s[:, c:c + LANES] - m_new) for c in range(0, n * t, LANES)], axis=1)
            corr = jnp.exp2(m - m_new)
            l_ref[hh] = corr * l_ref[hh] + jnp.sum(p, axis=1, keepdims=True)
            acc_ref[hh] = corr * acc_ref[hh] + _dot(p.astype(BF16), vb)
            m_ref[hh] = m_new

    done = 0
    n = ATT_A_NT
    while n >= 1:
        cnt = (i - done) // n

        def body(g, c, n=n, done=done):
            step(done + g * n, n, False)
            return c

        lax.fori_loop(0, cnt, body, 0)
        done = done + cnt * n
        n //= 2
    step(i, 1, True)
    lf = lam_ref[...]
    lam_full = (jnp.exp(jnp.sum(lf[0:1] * lf[1:2], axis=1, keepdims=True))
                - jnp.exp(jnp.sum(lf[2:3] * lf[3:4], axis=1, keepdims=True)) + lam_init)
    for hh in range(nh):
        o = acc_ref[hh] / l_ref[hh]
        d = o[:t] - lam_full * o[t:]
        y = d * lax.rsqrt(jnp.mean(d * d, axis=-1, keepdims=True) + LN_EPS) * g_ref[...]
        o_ref[:, hh * LANES:(hh + 1) * LANES] = (y * (1.0 - lam_init)).astype(o_ref.dtype)


def _diff_attn(qk, v_arr, v_col0, lam, subln_g, layer, batch, seq):
    t = min(ATT_A_T, seq)
    nh = ATT_A_HEADS
    w = nh * LANES
    assert H_A % nh == 0 and v_col0 % nh == 0
    lam_init = 0.8 - 0.6 * math.exp(-0.3 * layer)
    kern = functools.partial(_diff_attn_kernel, lam_init=lam_init)
    return pl.pallas_call(
        kern,
        grid=(batch, H_A // nh, seq // t),
        in_specs=[
            pl.BlockSpec((None, t, w), lambda b, h, i: (b, i, h)),
            pl.BlockSpec((None, seq, w), lambda b, h, i: (b, 0, H_A // nh + h)),
            pl.BlockSpec((None, seq, w), lambda b, h, i: (b, 0, v_col0 // nh + h)),
            pl.BlockSpec((4, HD_A), lambda b, h, i: (0, 0)),
            pl.BlockSpec((1, 2 * HD_A), lambda b, h, i: (0, 0)),
        ],
        out_specs=pl.BlockSpec((None, t, w), lambda b, h, i: (b, i, h)),
        out_shape=jax.ShapeDtypeStruct((batch, seq, A_W), BF16),
        scratch_shapes=[pltpu.VMEM((nh, 2 * t, LANES), BF16)] + [pltpu.VMEM((nh, 2 * t, LANES), F32)] * 3,
        compiler_params=_cparams(("parallel", "parallel", "parallel")),
        name="diff_attn",
    )(qk, qk, v_arr, lam, subln_g.reshape(1, 2 * HD_A))


def _stick_kernel(q_ref, k_ref, v_ref, o_ref, qq_ref, run_ref, acc_ref):
    tq = q_ref.shape[0]
    tk = min(ATT_C_TK, tq)
    per = tq // tk
    i = pl.program_id(2)
    npair = q_ref.shape[1] // LANES
    lane = lax.broadcasted_iota(jnp.int32, (tq, LANES), 1)
    for hp in range(npair):
        q = q_ref[:, hp * LANES:(hp + 1) * LANES]
        zero = jnp.zeros_like(q)
        qq_ref[hp] = jnp.concatenate([jnp.where(lane < HD_C, q, zero), jnp.where(lane < HD_C, zero, q)], axis=0)
    u = jnp.where(lax.broadcasted_iota(jnp.int32, (tk, tk), 0) >= lax.broadcasted_iota(jnp.int32, (tk, tk), 1),
                  -1.0, 0.0).astype(BF16)

    def front(hp, qs, j, tri):
        kb = k_ref[pl.ds(pl.multiple_of(j * tk, tk), tk), hp * LANES:(hp + 1) * LANES]
        z = _dot_nt(qs, kb)
        neg_abs = lax.bitcast_convert_type(
            lax.bitcast_convert_type(z, jnp.uint32) | jnp.uint32(0x80000000), F32)
        sp = jnp.maximum(z, 0.0) + jnp.log2(1.0 + jnp.exp2(neg_abs))
        if tri is not None:
            sp = jnp.where(tri, sp, 0.0)
        tail = _dot(sp.astype(BF16), u)
        return z + tail, tail[:, 0:1]

    def back(hp, j, w, run, tri):
        a = jnp.concatenate([jnp.exp2(w[:, c:c + LANES] + run) for c in range(0, tk, LANES)], axis=1)
        if tri is not None:
            a = jnp.where(tri, a, 0.0)
        return _dot(a.astype(BF16), v_ref[pl.ds(pl.multiple_of(j * tk, tk), tk), hp * LANES:(hp + 1) * LANES])

    def group(hp, qs, js, tris, run, acc, gates=None):
        fronts = [front(hp, qs, j, tri) for j, tri in zip(js, tris)]
        for n, (j, tri, (w, mass)) in enumerate(zip(js, tris, fronts)):
            pv = back(hp, j, w, run, tri)
            if gates is not None and gates[n] is not None:
                pv, mass = gates[n] * pv, gates[n] * mass
            acc = acc + pv
            run = run + mass
        return run, acc

    def full_group(first, n):
        for hp in range(npair):
            run, acc = group(hp, qq_ref[hp], [first - m for m in range(n)], [None] * n, run_ref[hp], acc_ref[hp])
            run_ref[hp] = run
            acc_ref[hp] = acc

    tri = (lax.broadcasted_iota(jnp.int32, (2 * tk, tk), 1)
           < lax.broadcasted_iota(jnp.int32, (2 * tk, tk), 0) % tk)
    zeros = jnp.zeros((2 * tk, LANES), F32)
    below = jnp.maximum(per * i - 1, 0)
    has_below = (i > 0).astype(F32)
    for hp in range(npair):
        for r in range(per):
            even, odd = pl.ds(r * tk, tk), pl.ds(tq + r * tk, tk)
            qs = jnp.concatenate([qq_ref[hp, even, :], qq_ref[hp, odd, :]], axis=0)
            run, acc = group(hp, qs, [per * i + r - m for m in range(r + 1)] + [below], [tri] + [None] * (r + 1),
                             zeros, zeros, [None] * (r + 1) + [has_below])
            run_ref[hp, even, :], run_ref[hp, odd, :] = run[:tk], run[tk:]
            acc_ref[hp, even, :], acc_ref[hp, odd, :] = acc[:tk], acc[tk:]

    def alive():
        return (jnp.max(run_ref[...]) > RUN_DEAD).astype(jnp.int32)

    def walk(first, n, live):
        count = (first + 1) // n

        def cond(c):
            g, live = c
            return jnp.logical_and(g < count, live > 0)

        def body(c):
            g, _ = c
            full_group(first - n * g, n)
            return g + 1, alive()

        g, live = lax.while_loop(cond, body, (jnp.int32(0), live))
        return first - n * g, live

    nxt, live = per * i - 2, alive()
    n = ATT_C_NT
    while n >= 1:
        nxt, live = walk(nxt, n, live)
        n //= 2
    for hp in range(npair):
        o_ref[:, hp * LANES:(hp + 1) * LANES] = jnp.where(
            lane < HD_C, acc_ref[hp, 0:tq, :], acc_ref[hp, tq:2 * tq, :]).astype(o_ref.dtype)


def _stick_breaking(arr, q_col0, k_col0, v_col0, batch, seq):
    t = min(ATT_C_T, seq)
    npair = ATT_C_PAIRS
    w = npair * LANES
    nhp = C_W // w
    assert C_W % w == 0 and q_col0 % npair == 0 and k_col0 % npair == 0 and v_col0 % npair == 0
    return pl.pallas_call(
        _stick_kernel,
        grid=(batch, nhp, seq // t),
        in_specs=[
            pl.BlockSpec((None, t, w), lambda b, h, i: (b, i, q_col0 // npair + h)),
            pl.BlockSpec((None, seq, w), lambda b, h, i: (b, 0, k_col0 // npair + h)),
            pl.BlockSpec((None, seq, w), lambda b, h, i: (b, 0, v_col0 // npair + h)),
        ],
        out_specs=pl.BlockSpec((None, t, w), lambda b, h, i: (b, i, h)),
        out_shape=jax.ShapeDtypeStruct((batch, seq, C_W), BF16),
        scratch_shapes=[pltpu.VMEM((npair, 2 * t, LANES), BF16)] + [pltpu.VMEM((npair, 2 * t, LANES), F32)] * 2,
        compiler_params=_cparams(("parallel", "parallel", "parallel")),
        name="stick_breaking",
    )(arr, arr, arr)


HALO = 8


def _ssd_kernel(z_ref, x_ref, bc_ref, dt_ref, cw_ref, cb_ref, dtb_ref, alog_ref, dskip_ref, ng_ref,
                o_ref, xe_ref, st_ref):
    L = z_ref.shape[0]
    c = pl.program_id(1)

    @pl.when(c == 0)
    def _():
        xe_ref[0:HALO, :] = jnp.zeros((HALO, CONV_DIM), F32)
        st_ref[...] = jnp.zeros(st_ref.shape, F32)

    xe_ref[HALO:HALO + L, 0:D_INNER] = x_ref[...]
    xe_ref[HALO:HALO + L, D_INNER:CONV_DIM] = bc_ref[...]
    conv = cb_ref[...]
    for r in range(CONV_B):
        off = HALO - (CONV_B - 1) + r
        conv = conv + cw_ref[r:r + 1, :] * xe_ref[off:off + L, :]
    xe_ref[0:HALO, :] = xe_ref[L:L + HALO, :]
    xc = conv * jax.nn.sigmoid(conv)
    xs = xc[:, :D_INNER]

    dt = _softplus(dt_ref[...] + dtb_ref[...])
    a = dt * (-jnp.exp(alog_ref[...]))
    row_i = lax.broadcasted_iota(jnp.int32, (L, L), 0)
    col_i = lax.broadcasted_iota(jnp.int32, (L, L), 1)
    causal = col_i <= row_i
    tri = jnp.where(causal, 1.0, 0.0).astype(BF16)
    acum = _dot_exact_rhs(tri, a)
    acum_t = acum.T
    a_last = acum[L - 1:L, :]
    e_in = jnp.exp(acum)
    e_out = jnp.exp(a_last - acum)

    expand = jnp.where(lax.broadcasted_iota(jnp.int32, (LANES, D_INNER), 0)
                       == lax.broadcasted_iota(jnp.int32, (LANES, D_INNER), 1) // P_B,
                       1.0, 0.0).astype(BF16)
    dt_x = _dot_exact_lhs(dt, expand)
    e_in_x = _dot_exact_lhs(e_in, expand)
    e_out_x = _dot_exact_lhs(e_out, expand)

    xdt = xs * dt_x
    xdt_b = xdt.astype(BF16)
    xdt_out_b = (xdt * e_out_x).astype(BF16)
    lane = lax.broadcasted_iota(jnp.int32, (L, LANES), 1)

    hg = H_B // G_B
    gw = D_INNER // G_B
    y_parts = []
    for g in range(G_B):
        bm_f = xc[:, D_INNER + g * N_B:D_INNER + (g + 1) * N_B]
        bm = bm_f.astype(BF16)
        bm_t = bm_f.T.astype(BF16)
        cm = xc[:, D_INNER + G_B * N_B + g * N_B:D_INNER + G_B * N_B + (g + 1) * N_B].astype(BF16)
        cb = _dot_nt(cm, bm)
        st_g = st_ref[:, g * gw:(g + 1) * gw]
        y_off = _dot(cm, st_g.astype(BF16)) * e_in_x[:, g * gw:(g + 1) * gw]
        y_diag = []
        for hp in range(hg // 2):
            h0 = g * hg + 2 * hp
            pair = xdt_b[:, h0 * P_B:(h0 + 2) * P_B]
            ys = []
            for h in (h0, h0 + 1):
                seg = acum[:, h:h + 1] - acum_t[h:h + 1, :]
                m = (cb * jnp.exp(jnp.where(causal, seg, -jnp.inf))).astype(BF16)
                ys.append(_dot(m, pair))
            y_diag.append(jnp.where(lane < P_B, ys[0], ys[1]))
        y_parts.append(jnp.concatenate(y_diag, axis=1) + y_off)
        upd = _dot(bm_t, xdt_out_b[:, g * gw:(g + 1) * gw])
        st_ref[:, g * gw:(g + 1) * gw] = st_g * e_in_x[L - 1:L, g * gw:(g + 1) * gw] + upd

    y = jnp.concatenate(y_parts, axis=1) + dskip_ref[...] * xs
    zz = z_ref[...]
    y = y * (zz * jax.nn.sigmoid(zz))
    outs = []
    for g in range(G_B):
        yg = y[:, g * gw:(g + 1) * gw]
        outs.append(yg * lax.rsqrt(jnp.mean(yg * yg, axis=-1, keepdims=True) + LN_EPS))
    o_ref[...] = (jnp.concatenate(outs, axis=1) * ng_ref[...]).astype(o_ref.dtype)


def _ssd(ssm, conv_w, conv_b, dt_bias, a_log, d_skip, norm_g, batch, seq):
    L = min(SSD_L, seq)
    pad = LANES - H_B
    row = lambda v: v.reshape(1, -1).astype(F32)
    dtb = jnp.pad(row(dt_bias), ((0, 0), (0, pad)))
    alog = jnp.pad(row(a_log), ((0, 0), (0, pad)))
    dskip = jnp.repeat(row(d_skip), P_B, axis=1)
    const = lambda b, c: (0, 0)
    bc_w = CONV_DIM - D_INNER
    return pl.pallas_call(
        _ssd_kernel,
        grid=(batch, seq // L),
        in_specs=[
            pl.BlockSpec((None, L, D_INNER), lambda b, c: (b, c, 0)),
            pl.BlockSpec((None, L, D_INNER), lambda b, c: (b, c, 1)),
            pl.BlockSpec((None, L, bc_w), lambda b, c: (b, c, 2 * D_INNER // bc_w)),
            pl.BlockSpec((None, L, LANES), lambda b, c: (b, c, (D_INNER + CONV_DIM) // LANES)),
            pl.BlockSpec((CONV_B, CONV_DIM), const),
            pl.BlockSpec((1, CONV_DIM), const),
            pl.BlockSpec((1, LANES), const),
            pl.BlockSpec((1, LANES), const),
            pl.BlockSpec((1, D_INNER), const),
            pl.BlockSpec((1, D_INNER), const),
        ],
        out_specs=pl.BlockSpec((None, L, D_INNER), lambda b, c: (b, c, 0)),
        out_shape=jax.ShapeDtypeStruct((batch, seq, D_INNER), BF16),
        scratch_shapes=[pltpu.VMEM((HALO + L, CONV_DIM), F32), pltpu.VMEM((N_B, D_INNER), F32)],
        compiler_params=_cparams(("parallel", "arbitrary")),
        name="ssd",
    )(ssm, ssm, ssm, ssm, conv_w, row(conv_b), dtb, alog, dskip, row(norm_g))


def _merge_ln_kernel(h_ref, oa_ref, ob_ref, oc_ref, wg_ref, gb_ref, wbr_ref, wout_ref, g_ref, b_ref, o_ref):
    h = h_ref[...]
    hb = h.astype(BF16)
    merged = jnp.zeros(h.shape, F32)
    for r, br_ref in enumerate((oa_ref, ob_ref, oc_ref)):
        gpre = _dot(hb, wg_ref[:, r * D_MODEL:(r + 1) * D_MODEL]) + gb_ref[r:r + 1, :]
        merged = merged + jax.nn.sigmoid(gpre) * _dot(br_ref[...], wbr_ref[r])
    mix = _dot(merged.astype(BF16), wout_ref[...])
    o_ref[...] = _layernorm(DEEPNORM_ALPHA * h + mix, g_ref[...], b_ref[...])


def _merge_ln(h, oa, ob, oc, w_gate, gate_bias, w_branch, w_out, g, b):
    t = h.shape[0]
    tm = min(MERGE_TM, t)
    rowspec = pl.BlockSpec((tm, D_MODEL), lambda i: (i, 0))
    const2 = lambda i: (0, 0)
    one = pl.Buffered(1)
    return pl.pallas_call(
        _merge_ln_kernel,
        grid=(t // tm,),
        in_specs=[
            rowspec, rowspec, rowspec, rowspec,
            pl.BlockSpec((D_MODEL, N_BRANCH * D_MODEL), const2, pipeline_mode=one),
            pl.BlockSpec((N_BRANCH, D_MODEL), const2),
            pl.BlockSpec((N_BRANCH, D_MODEL, D_MODEL), lambda i: (0, 0, 0), pipeline_mode=one),
            pl.BlockSpec((D_MODEL, D_MODEL), const2, pipeline_mode=one),
            pl.BlockSpec((1, D_MODEL), const2),
            pl.BlockSpec((1, D_MODEL), const2),
        ],
        out_specs=rowspec,
        out_shape=jax.ShapeDtypeStruct((t, D_MODEL), F32),
        compiler_params=_cparams(("parallel",)),
        name="merge_ln",
    )(h, oa, ob, oc, w_gate, gate_bias, w_branch, w_out, g.reshape(1, D_MODEL), b.reshape(1, D_MODEL))


def _mixer_weights(w_in):
    segs = (A_W, A_W, A_W, D_INNER, CONV_DIM, H_B, C_W, C_W, C_W, N_BRANCH * D_MODEL)
    offs = [0]
    for s in segs:
        offs.append(offs[-1] + s)
    qa, ka, va, zb, xbc, dtb, qc, kc, vc, gate = (w_in[:, offs[n]:offs[n + 1]] for n in range(len(segs)))
    w_qk = jnp.concatenate([qa, ka], axis=1).astype(BF16)
    w_att = jnp.concatenate([va, qc, kc, vc], axis=1).astype(BF16)
    w_ssm = jnp.concatenate([zb, xbc, jnp.pad(dtb, ((0, 0), (0, LANES - H_B)))], axis=1).astype(BF16)
    return w_qk, w_att, w_ssm, gate.astype(BF16)


def kernel(x, ffn1_w_in, ffn1_w_out, ln1_g, ln1_b, w_mix_in, gate_bias, diff_lambda, diff_subln_g,
           ssm_conv_w, ssm_conv_b, ssm_dt_bias, ssm_A_log, ssm_D, ssm_norm_g, w_branch, w_mix_out,
           ln2_g, ln2_b, ffn2_w_in, ffn2_w_out, ln3_g, ln3_b):
    batch, seq, _ = x.shape
    t = batch * seq
    tabs = _rope_tables(seq)
    xf = x.reshape(t, D_MODEL)
    for l in range(DEPTH):
        xf = _ffn_ln(xf, ffn1_w_in[l].astype(BF16), ffn1_w_out[l].astype(BF16), ln1_g[l], ln1_b[l])

        w_qk, w_att, w_ssm, w_gate = _mixer_weights(w_mix_in[l])
        qk, att = _proj_attn(xf, w_qk, w_att, tabs, seq)
        qk = qk.reshape(batch, seq, 2 * A_W)
        att = att.reshape(batch, seq, A_W + 3 * C_W)
        ssm = _proj_ssm(xf, w_ssm).reshape(batch, seq, w_ssm.shape[1])

        o_a = _diff_attn(qk, att, 0, diff_lambda[l], diff_subln_g[l], l, batch, seq)
        nb = A_W // LANES
        o_c = _stick_breaking(att, nb, 2 * nb, 3 * nb, batch, seq)
        o_b = _ssd(ssm, ssm_conv_w[l], ssm_conv_b[l], ssm_dt_bias[l], ssm_A_log[l], ssm_D[l],
                   ssm_norm_g[l], batch, seq)

        xf = _merge_ln(xf, o_a.reshape(t, A_W), o_b.reshape(t, D_INNER), o_c.reshape(t, C_W),
                       w_gate, gate_bias[l], w_branch[l].astype(BF16), w_mix_out[l].astype(BF16),
                       ln2_g[l], ln2_b[l])

        xf = _ffn_ln(xf, ffn2_w_in[l].astype(BF16), ffn2_w_out[l].astype(BF16), ln3_g[l], ln3_b[l])
    return xf.reshape(batch, seq, D_MODEL)
```
